```python
import math
import jax, jax.numpy as jnp
from jax import lax
import numpy as np

D_MODEL = 1024
BATCH = 16
SEQ = 2048
DEPTH = 2

N_A = max(1, DEPTH // 2)
N_B = DEPTH - N_A

D_RNN = 1280
RNN_BLOCKS = 10
RNN_BW = D_RNN // RNN_BLOCKS
CONV_WIDTH = 4
LRU_C = 8.0

N_HEADS = 8
QK_NOPE = 128
QK_ROPE = 64
V_DIM = 128
KV_RANK = 256
Q_RANK = 384
ROPE_THETA = 10000.0
Q_BLOCK = 128
ATTN_SCALE = (QK_NOPE + QK_ROPE) ** -0.5
EPS = 1e-6

kernel_name = "yoco_rglru_mla_hybrid"


def rms_norm(x, g):
    xf = x.astype(jnp.float32)
    y = xf * lax.rsqrt(jnp.mean(xf * xf, axis=-1, keepdims=True) + EPS)
    return (y * g.astype(jnp.float32)).astype(x.dtype)


def rope_tables(seq_len):
    pos = jnp.arange(seq_len, dtype=jnp.float32)
    inv = ROPE_THETA ** (-jnp.arange(0, QK_ROPE, 2, dtype=jnp.float32) / QK_ROPE)
    ang = pos[:, None] * inv[None, :]
    return jnp.cos(ang), jnp.sin(ang)


def apply_rope(x, cos, sin):
    xf = x.astype(jnp.float32)
    x1, x2 = jnp.split(xf, 2, axis=-1)
    out = jnp.concatenate([x1 * cos - x2 * sin, x2 * cos + x1 * sin], axis=-1)
    return out.astype(x.dtype)


def causal_depthwise_conv(x, w, b):
    c = x.shape[-1]
    y = lax.conv_general_dilated(
        x, w[:, None, :].astype(x.dtype), window_strides=(1,),
        padding=[(CONV_WIDTH - 1, 0)], dimension_numbers=("NWC", "WIO", "NWC"),
        feature_group_count=c)
    return y + b


def _lin_combine(left, right):
    a1, b1 = left
    a2, b2 = right
    return a1 * a2, a2 * b1 + b2


def rglru_layer(x, norm_g, w_in, conv_w, conv_b, w_rg, b_rg, w_ig, b_ig, lam, w_out):
    bsz, seq, _ = x.shape
    h = rms_norm(x, norm_g)
    u = h @ w_in
    xb, gate = u[..., :D_RNN], u[..., D_RNN:]
    xb = causal_depthwise_conv(xb, conv_w, conv_b)
    xblk = xb.reshape(bsz, seq, RNN_BLOCKS, RNN_BW)
    r = jax.nn.sigmoid(jnp.einsum("bsnc,ncd->bsnd", xblk, w_rg).reshape(bsz, seq, D_RNN) + b_rg)
    i = jax.nn.sigmoid(jnp.einsum("bsnc,ncd->bsnd", xblk, w_ig).reshape(bsz, seq, D_RNN) + b_ig)
    log_a = -LRU_C * r.astype(jnp.float32) * jax.nn.softplus(-lam.astype(jnp.float32))
    a = jnp.exp(log_a)
    bterm = jnp.sqrt(-jnp.expm1(2.0 * log_a)) * (i * xb).astype(jnp.float32)
    _, hs = lax.associative_scan(_lin_combine, (a, bterm), axis=1)
    y = hs.astype(x.dtype) * jax.nn.silu(gate)
    return y @ w_out


def mla_shared_kv(x_stream, norm_kv, w_dkv, kv_norm, w_uk, w_uv, cos, sin):
    h = rms_norm(x_stream, norm_kv)
    ckr = h @ w_dkv
    c_kv = rms_norm(ckr[..., :KV_RANK], kv_norm)
    k_rope = apply_rope(ckr[..., KV_RANK:], cos[None], sin[None])
    k_nope = jnp.einsum("bsc,chd->bshd", c_kv, w_uk)
    v = jnp.einsum("bsc,chd->bshd", c_kv, w_uv)
    return k_nope, k_rope, v


def causal_block_attention(q_nope, q_rope, k_nope, k_rope, v):
    bsz, seq, nh, _ = q_nope.shape
    nb = seq // Q_BLOCK
    qn = q_nope.reshape(bsz, nb, Q_BLOCK, nh, QK_NOPE).transpose(1, 0, 2, 3, 4)
    qr = q_rope.reshape(bsz, nb, Q_BLOCK, nh, QK_ROPE).transpose(1, 0, 2, 3, 4)
    kpos = jnp.arange(seq)

    def one_block(args):
        qn_b, qr_b, bi = args
        s = jnp.einsum("bqhd,bkhd->bhqk", qn_b, k_nope, preferred_element_type=jnp.float32)
        s = s + jnp.einsum("bqhr,bkr->bhqk", qr_b, k_rope, preferred_element_type=jnp.float32)
        s = s * ATTN_SCALE
        qpos = bi * Q_BLOCK + jnp.arange(Q_BLOCK)
        mask = kpos[None, :] <= qpos[:, None]
        s = jnp.where(mask[None, None], s, -jnp.inf)
        p = jax.nn.softmax(s, axis=-1)
        return jnp.einsum("bhqk,bkhd->bqhd", p.astype(v.dtype), v)

    o = lax.map(one_block, (qn, qr, jnp.arange(nb)))
    return o.transpose(1, 0, 2, 3, 4).reshape(bsz, seq, nh, V_DIM)


def mla_layer(x, norm_g, w_in, q_norm, w_uq, w_out, k_nope, k_rope, v, cos, sin):
    bsz, seq, _ = x.shape
    h = rms_norm(x, norm_g)
    u = h @ w_in
    c_q = rms_norm(u[..., :Q_RANK], q_norm)
    gate = u[..., Q_RANK:]
    q = jnp.einsum("bsc,chd->bshd", c_q, w_uq)
    q_nope = q[..., :QK_NOPE]
    q_rope = apply_rope(q[..., QK_NOPE:], cos[None, :, None], sin[None, :, None])
    o = causal_block_attention(q_nope, q_rope, k_nope, k_rope, v)
    y = o.reshape(bsz, seq, N_HEADS * V_DIM) * jax.nn.silu(gate)
    return y @ w_out


def setup_inputs(seed: int = 0) -> dict:
    key = jax.random.key(seed)
    ks = jax.random.split(key, 24)
    f32 = jnp.float32
    nrm = lambda k, shape, fan_in: jax.random.normal(k, shape, f32) * (fan_in ** -0.5)
    gain = lambda k, shape: 1.0 + 0.02 * jax.random.normal(k, shape, f32)
    small = lambda k, shape: 0.01 * jax.random.normal(k, shape, f32)

    u = jax.random.uniform(ks[8], (N_A, D_RNN), f32, 0.9, 0.999)
    a0 = u ** (1.0 / LRU_C)
    lam = jnp.log(a0) - jnp.log1p(-a0)

    return {
        "x": jax.random.normal(ks[0], (BATCH, SEQ, D_MODEL), f32),
        "norm_a": gain(ks[1], (N_A, D_MODEL)),
        "w_in_a": nrm(ks[2], (N_A, D_MODEL, 2 * D_RNN), D_MODEL),
        "conv_w": nrm(ks[3], (N_A, CONV_WIDTH, D_RNN), CONV_WIDTH),
        "conv_b": small(ks[4], (N_A, D_RNN)),
        "w_rg": nrm(ks[5], (N_A, RNN_BLOCKS, RNN_BW, RNN_BW), RNN_BW),
        "b_rg": small(ks[6], (N_A, D_RNN)),
        "w_ig": nrm(ks[7], (N_A, RNN_BLOCKS, RNN_BW, RNN_BW), RNN_BW),
        "b_ig": small(ks[9], (N_A, D_RNN)),
        "lru_lambda": lam,
        "w_out_a": nrm(ks[10], (N_A, D_RNN, D_MODEL), D_RNN),
        "norm_kv": gain(ks[11], (D_MODEL,)),
        "w_dkv": nrm(ks[12], (D_MODEL, KV_RANK + QK_ROPE), D_MODEL),
        "kv_norm": gain(ks[13], (KV_RANK,)),
        "w_uk": nrm(ks[14], (KV_RANK, N_HEADS, QK_NOPE), KV_RANK),
        "w_uv": nrm(ks[15], (KV_RANK, N_HEADS, V_DIM), KV_RANK),
        "norm_b": gain(ks[16], (N_B, D_MODEL)),
        "w_in_b": nrm(ks[17], (N_B, D_MODEL, Q_RANK + N_HEADS * V_DIM), D_MODEL),
        "q_norm": gain(ks[18], (N_B, Q_RANK)),
        "w_uq": nrm(ks[19], (N_B, Q_RANK, N_HEADS, QK_NOPE + QK_ROPE), Q_RANK),
        "w_out_b": nrm(ks[20], (N_B, N_HEADS * V_DIM, D_MODEL), N_HEADS * V_DIM),
        "final_norm": gain(ks[21], (D_MODEL,)),
    }


def reference(x, norm_a, w_in_a, conv_w, conv_b, w_rg, b_rg, w_ig, b_ig, lru_lambda, w_out_a,
              norm_kv, w_dkv, kv_norm, w_uk, w_uv,
              norm_b, w_in_b, q_norm, w_uq, w_out_b, final_norm):
    seq = x.shape[1]
    cos, sin = rope_tables(seq)
    k_nope = k_rope = v = None
    for layer in range(DEPTH):
        if layer < N_A:
            x = x + rglru_layer(x, norm_a[layer], w_in_a[layer], conv_w[layer], conv_b[layer],
                                w_rg[layer], b_rg[layer], w_ig[layer], b_ig[layer],
                                lru_lambda[layer], w_out_a[layer])
        else:
            if layer == N_A:
                k_nope, k_rope, v = mla_shared_kv(x, norm_kv, w_dkv, kv_norm, w_uk, w_uv, cos, sin)
            j = layer - N_A
            x = x + mla_layer(x, norm_b[j], w_in_b[j], q_norm[j], w_uq[j], w_out_b[j],
                              k_nope, k_rope, v, cos, sin)
    return rms_norm(x, final_norm)
```

```python
import functools

import jax
import jax.numpy as jnp
from jax import lax
from jax.experimental import pallas as pl
from jax.experimental.pallas import tpu as pltpu

F32 = jnp.float32
BF16 = jnp.bfloat16

EPS = 1e-6
LRU_C = 8.0
ROPE_THETA = 10000.0
CONV_WIDTH = 4
NEG_BIG = -1e30

VMEM_LIMIT_BYTES = 56 * 1024 * 1024

RNN_COLS = 256
GATE_BW = 128
TIME_CHUNK = 32
PROJ_ROWS = 512
ATTN_TQ = 256


def _rms(x, g):
    ms = jnp.mean(x * x, axis=-1, keepdims=True)
    return x * lax.rsqrt(ms + EPS) * g


def _sigmoid(x):
    return 1.0 / (1.0 + jnp.exp(-x))


def _rglru_kernel(x_ref, norm_ref, winx_ref, wing_ref, convw_ref, convb_ref, wg_ref,
                  brg_ref, big_ref, lam_ref, wout_ref, o_ref,
                  h_scr, xbuf, y_scr, hist, hstate, *, batch, tchunk, ncol):
    rows = batch * tchunk
    halo = (CONV_WIDTH - 1) * batch

    @pl.when(pl.program_id(0) == 0)
    def _():
        hist[...] = jnp.zeros_like(hist)
        hstate[...] = jnp.zeros_like(hstate)

    x = x_ref[...]
    h_scr[...] = _rms(x, norm_ref[...]).astype(BF16)

    def col_block(j, carry):
        hb = h_scr[...]
        xb = jnp.dot(hb, winx_ref[j], preferred_element_type=F32)
        xbuf[0:halo, :] = hist[j]
        xbuf[halo:halo + rows, :] = xb
        hist[j] = xb[rows - halo:, :]
        cw = convw_ref[j]
        xc = convb_ref[j] + cw[3:4, :] * xb
        for k in range(CONV_WIDTH - 1):
            xc = xc + cw[k:k + 1, :] * xbuf[k * batch:k * batch + rows, :]
        xcb = xc.astype(BF16)
        g0 = jnp.dot(xcb[:, :GATE_BW], wg_ref[2 * j], preferred_element_type=F32)
        g1 = jnp.dot(xcb[:, GATE_BW:], wg_ref[2 * j + 1], preferred_element_type=F32)
        r_pre = jnp.concatenate([g0[:, :GATE_BW], g1[:, :GATE_BW]], axis=1) + brg_ref[j]
        i_pre = jnp.concatenate([g0[:, GATE_BW:], g1[:, GATE_BW:]], axis=1) + big_ref[j]
        r = _sigmoid(r_pre)
        ig = _sigmoid(i_pre)
        z = -lam_ref[j]
        softplus = jnp.maximum(z, 0.0) + jnp.log1p(jnp.exp(-jnp.abs(z)))
        log_a = r * (-LRU_C * softplus)
        a = jnp.exp(log_a)
        bt = jnp.sqrt(1.0 - a * a) * (ig * xc)
        hst = hstate[j]
        hs = []
        for t in range(tchunk):
            sl = slice(t * batch, (t + 1) * batch)
            hst = a[sl, :] * hst + bt[sl, :]
            hs.append(hst)
        hstate[j] = hst
        hs = jnp.concatenate(hs, axis=0)
        gate = jnp.dot(hb, wing_ref[j], preferred_element_type=F32)
        y_scr[j] = (hs * (gate * _sigmoid(gate))).astype(BF16)
        return carry

    lax.fori_loop(0, ncol, col_block, 0)

    acc = x
    for j in range(ncol):
        acc = acc + jnp.dot(y_scr[j], wout_ref[j], preferred_element_type=F32)
    o_ref[...] = acc


def _rglru_layer(x_tb, norm_g, w_in, conv_w, conv_b, w_rg, b_rg, w_ig, b_ig, lam, w_out,
                 *, batch):
    n_rows, d_model = x_tb.shape
    d_rnn = w_out.shape[0]
    n_blocks = w_rg.shape[0]
    assert d_rnn % RNN_COLS == 0 and w_rg.shape[1] == GATE_BW
    ncol = d_rnn // RNN_COLS
    rows = batch * TIME_CHUNK
    assert n_rows % rows == 0
    halo = (CONV_WIDTH - 1) * batch

    def cols(w):
        lead = w.shape[:-1]
        w = w.reshape(lead + (ncol, RNN_COLS))
        return jnp.moveaxis(w, -2, 0)

    winx = cols(w_in[:, :d_rnn]).astype(BF16)
    wing = cols(w_in[:, d_rnn:]).astype(BF16)
    convw = cols(conv_w)
    convb = cols(conv_b[None, :])
    wg = jnp.concatenate([w_rg, w_ig], axis=-1).astype(BF16)
    brg = cols(b_rg[None, :])
    big = cols(b_ig[None, :])
    lamc = cols(lam[None, :])
    wout = w_out.reshape(ncol, RNN_COLS, d_model).astype(BF16)

    full = lambda a: pl.BlockSpec(a.shape, lambda i: (0,) * a.ndim)
    kern = functools.partial(_rglru_kernel, batch=batch, tchunk=TIME_CHUNK, ncol=ncol)
    return pl.pallas_call(
        kern,
        out_shape=jax.ShapeDtypeStruct((n_rows, d_model), F32),
        grid=(n_rows // rows,),
        in_specs=[
            pl.BlockSpec((rows, d_model), lambda i: (i, 0)),
            full(norm_g), full(winx), full(wing), full(convw), full(convb), full(wg),
            full(brg), full(big), full(lamc), full(wout),
        ],
        out_specs=pl.BlockSpec((rows, d_model), lambda i: (i, 0)),
        scratch_shapes=[
            pltpu.VMEM((rows, d_model), BF16),
            pltpu.VMEM((halo + rows, RNN_COLS), F32),
            pltpu.VMEM((ncol, rows, RNN_COLS), BF16),
            pltpu.VMEM((ncol, halo, RNN_COLS), F32),
            pltpu.VMEM((ncol, batch, RNN_COLS), F32),
        ],
        compiler_params=pltpu.CompilerParams(
            dimension_semantics=("arbitrary",), vmem_limit_bytes=VMEM_LIMIT_BYTES),
        name="rglru_layer",
    )(x_tb, norm_g, winx, wing, convw, convb, wg, brg, big, lamc, wout)


def _proj_kernel(x_ref, cs_ref, nkv_ref, nb_ref, wdkv_ref, kvn_ref, wuk_ref, wuv_ref,
                 winq_ref, wing_ref, qn_ref, wuq_ref,
                 q_ref, k_ref, v_ref, sg_ref, *, n_heads, kv_rank, nope, rope, v_dim):
    x = x_ref[...]
    ms = jnp.mean(x * x, axis=-1, keepdims=True)
    xn = x * lax.rsqrt(ms + EPS)
    cs = cs_ref[...]

    def roped(tile):
        p = tile * cs
        return (p + pltpu.roll(p, rope, axis=1))[:, :rope]

    hk = (xn * nkv_ref[...]).astype(BF16)
    ckr = jnp.dot(hk, wdkv_ref[...], preferred_element_type=F32)
    c_kv = _rms(ckr[:, :kv_rank], kvn_ref[...]).astype(BF16)
    k_rope = roped(ckr[:, kv_rank:]).astype(BF16)
    k_nope = jnp.dot(c_kv, wuk_ref[...], preferred_element_type=F32)
    v = jnp.dot(c_kv, wuv_ref[...], preferred_element_type=F32)
    for h in range(n_heads):
        k_ref[h, :, 0:nope] = k_nope[:, h * nope:(h + 1) * nope].astype(BF16)
        k_ref[h, :, nope:nope + rope] = k_rope
        v_ref[h] = v[:, h * v_dim:(h + 1) * v_dim].astype(BF16)

    hq = (xn * nb_ref[...]).astype(BF16)
    cq = jnp.dot(hq, winq_ref[...], preferred_element_type=F32)
    c_q = _rms(cq, qn_ref[...]).astype(BF16)
    gate = jnp.dot(hq, wing_ref[...], preferred_element_type=F32)
    sg_ref[...] = (gate * _sigmoid(gate)).astype(BF16)
    qw = nope + 2 * rope
    for h in range(n_heads):
        qh = jnp.dot(c_q, wuq_ref[:, h * qw:(h + 1) * qw], preferred_element_type=F32)
        q_ref[h, :, 0:nope] = qh[:, :nope].astype(BF16)
        q_ref[h, :, nope:nope + rope] = roped(qh[:, nope:]).astype(BF16)


def _swap_halves(w):
    half = w.shape[-1] // 2
    return jnp.concatenate([w[..., half:], w[..., :half]], axis=-1)


def _mla_proj(x1, cs, norm_kv, w_dkv, kv_norm, w_uk, w_uv, norm_b, w_in, q_norm, w_uq):
    bsz, seq, d_model = x1.shape
    kv_rank = kv_norm.shape[0]
    rope = w_dkv.shape[1] - kv_rank
    _, n_heads, nope = w_uk.shape
    v_dim = w_uv.shape[2]
    q_rank = q_norm.shape[0]
    assert 2 * rope == 128 and nope % 128 == 0 and seq % PROJ_ROWS == 0

    wr = w_dkv[:, kv_rank:]
    wdkv = jnp.concatenate([w_dkv[:, :kv_rank], wr, _swap_halves(wr)], axis=1).astype(BF16)
    wuk = w_uk.reshape(kv_rank, n_heads * nope).astype(BF16)
    wuv = w_uv.reshape(kv_rank, n_heads * v_dim).astype(BF16)
    winq = w_in[:, :q_rank].astype(BF16)
    wing = w_in[:, q_rank:].astype(BF16)
    wqr = w_uq[:, :, nope:]
    wuq = jnp.concatenate([w_uq[:, :, :nope], wqr, _swap_halves(wqr)], axis=-1)
    wuq = wuq.reshape(q_rank, n_heads * (nope + 2 * rope)).astype(BF16)

    r2 = lambda a: a.reshape(1, -1)
    args = (x1, cs, r2(norm_kv), r2(norm_b), wdkv, r2(kv_norm), wuk, wuv, winq, wing,
            r2(q_norm), wuq)
    full = lambda a: pl.BlockSpec(a.shape, lambda b, i: (0,) * a.ndim)
    qk = nope + rope
    kern = functools.partial(_proj_kernel, n_heads=n_heads, kv_rank=kv_rank, nope=nope,
                             rope=rope, v_dim=v_dim)
    head_out = lambda d: pl.BlockSpec((None, n_heads, PROJ_ROWS, d), lambda b, i: (b, 0, i, 0))
    return pl.pallas_call(
        kern,
        out_shape=(
            jax.ShapeDtypeStruct((bsz, n_heads, seq, qk), BF16),
            jax.ShapeDtypeStruct((bsz, n_heads, seq, qk), BF16),
            jax.ShapeDtypeStruct((bsz, n_heads, seq, v_dim), BF16),
            jax.ShapeDtypeStruct((bsz, seq, n_heads * v_dim), BF16),
        ),
        grid=(bsz, seq // PROJ_ROWS),
        in_specs=[
            pl.BlockSpec((None, PROJ_ROWS, d_model), lambda b, i: (b, i, 0)),
            pl.BlockSpec((PROJ_ROWS, 4 * (rope // 2)), lambda b, i: (i, 0)),
        ] + [full(a) for a in args[2:]],
        out_specs=(
            head_out(qk), head_out(qk), head_out(v_dim),
            pl.BlockSpec((None, PROJ_ROWS, n_heads * v_dim), lambda b, i: (b, i, 0)),
        ),
        compiler_params=pltpu.CompilerParams(
            dimension_semantics=("parallel", "parallel"), vmem_limit_bytes=VMEM_LIMIT_BYTES),
        name="mla_proj",
    )(*args)


def _attn_kernel(q_ref, k_ref, v_ref, sg_ref, x_ref, wout_ref, fn_ref, o_ref, o_scr,
                 *, n_heads, tq, scale):
    qi = pl.program_id(1)
    row = lax.broadcasted_iota(jnp.int32, (tq, tq), 0)
    col = lax.broadcasted_iota(jnp.int32, (tq, tq), 1)
    causal = col <= row

    def head_body(h, carry):
        q = q_ref[h]

        def tile(j, m, l, acc, masked):
            start = pl.multiple_of(j * tq, tq)
            k = k_ref[h, pl.ds(start, tq), :]
            v = v_ref[h, pl.ds(start, tq), :]
            s = lax.dot_general(q, k, (((1,), (1,)), ((), ())),
                                preferred_element_type=F32) * scale
            if masked:
                s = jnp.where(causal, s, NEG_BIG)
            m_new = jnp.maximum(m, jnp.max(s, axis=-1, keepdims=True))
            alpha = jnp.exp(m - m_new)
            p = jnp.exp(s - m_new)
            l = alpha * l + jnp.sum(p, axis=-1, keepdims=True)
            acc = alpha * acc + jnp.dot(p.astype(BF16), v, preferred_element_type=F32)
            return m_new, l, acc

        init = (jnp.full((tq, 1), NEG_BIG, F32), jnp.zeros((tq, 1), F32),
                jnp.zeros((tq, v_ref.shape[-1]), F32))
        m, l, acc = lax.fori_loop(0, qi, lambda j, c: tile(j, *c, masked=False), init)
        m, l, acc = tile(qi, m, l, acc, masked=True)
        o_scr[h] = acc / l
        return carry

    lax.fori_loop(0, n_heads, head_body, 0)

    o = jnp.concatenate([o_scr[h] for h in range(n_heads)], axis=1)
    y = (o * sg_ref[...].astype(F32)).astype(BF16)
    out = x_ref[...] + jnp.dot(y, wout_ref[...], preferred_element_type=F32)
    o_ref[...] = _rms(out, fn_ref[...])


def _mla_attn(q, k, v, sg, x1, w_out, final_norm, *, scale):
    bsz, n_heads, seq, qk = q.shape
    v_dim = v.shape[-1]
    d_model = x1.shape[-1]
    tq = ATTN_TQ
    assert seq % tq == 0
    kern = functools.partial(_attn_kernel, n_heads=n_heads, tq=tq, scale=scale)
    wout = w_out.astype(BF16)
    fn = final_norm.reshape(1, -1)
    return pl.pallas_call(
        kern,
        out_shape=jax.ShapeDtypeStruct((bsz, seq, d_model), F32),
        grid=(bsz, seq // tq),
        in_specs=[
            pl.BlockSpec((None, n_heads, tq, qk), lambda b, i: (b, 0, i, 0)),
            pl.BlockSpec((None, n_heads, seq, qk), lambda b, i: (b, 0, 0, 0)),
            pl.BlockSpec((None, n_heads, seq, v_dim), lambda b, i: (b, 0, 0, 0)),
            pl.BlockSpec((None, tq, n_heads * v_dim), lambda b, i: (b, i, 0)),
            pl.BlockSpec((None, tq, d_model), lambda b, i: (b, i, 0)),
            pl.BlockSpec(wout.shape, lambda b, i: (0, 0)),
            pl.BlockSpec(fn.shape, lambda b, i: (0, 0)),
        ],
        out_specs=pl.BlockSpec((None, tq, d_model), lambda b, i: (b, i, 0)),
        scratch_shapes=[pltpu.VMEM((n_heads, tq, v_dim), F32)],
        compiler_params=pltpu.CompilerParams(
            dimension_semantics=("parallel", "arbitrary"), vmem_limit_bytes=VMEM_LIMIT_BYTES),
        name="mla_attn",
    )(q, k, v, sg, x1, wout, fn)


def _rope_table(seq, rope):
    pos = jnp.arange(seq, dtype=F32)
    inv = ROPE_THETA ** (-jnp.arange(0, rope, 2, dtype=F32) / rope)
    ang = pos[:, None] * inv[None, :]
    cos, sin = jnp.cos(ang), jnp.sin(ang)
    return jnp.concatenate([cos, cos, -sin, sin], axis=1)


def kernel(x, norm_a, w_in_a, conv_w, conv_b, w_rg, b_rg, w_ig, b_ig, lru_lambda, w_out_a,
           norm_kv, w_dkv, kv_norm, w_uk, w_uv, norm_b, w_in_b, q_norm, w_uq, w_out_b,
           final_norm):
    bsz, seq, d_model = x.shape
    n_a = w_in_a.shape[0]
    n_b = w_in_b.shape[0]
    assert n_b == 1, "the final norm is fused into the (single) MLA layer"
    nope = w_uk.shape[-1]
    rope = w_dkv.shape[1] - kv_norm.shape[0]
    scale = float(nope + rope) ** -0.5

    x_tb = jnp.transpose(x, (1, 0, 2)).reshape(seq * bsz, d_model)
    for layer in range(n_a):
        x_tb = _rglru_layer(x_tb, norm_a[layer].reshape(1, -1), w_in_a[layer], conv_w[layer],
                            conv_b[layer], w_rg[layer], b_rg[layer], w_ig[layer], b_ig[layer],
                            lru_lambda[layer], w_out_a[layer], batch=bsz)
    x1 = jnp.transpose(x_tb.reshape(seq, bsz, d_model), (1, 0, 2))

    cs = _rope_table(seq, rope)
    q, k, v, sg = _mla_proj(x1, cs, norm_kv, w_dkv, kv_norm, w_uk, w_uv, norm_b[0],
                            w_in_b[0], q_norm[0], w_uq[0])
    return _mla_attn(q, k, v, sg, x1, w_out_b[0], final_norm, scale=scale)
```

```python
import functools

import jax
import jax.numpy as jnp
from jax import lax
from jax.experimental import pallas as pl
from jax.experimental.pallas import tpu as pltpu

F32 = jnp.float32
BF16 = jnp.bfloat16

EPS = 1e-6
LRU_C = 8.0
ROPE_THETA = 10000.0
CONV_WIDTH = 4
NEG_BIG = -1e30
LOG2_E = 1.4426950408889634
LANES = 128

VMEM_LIMIT_BYTES = 56 * 1024 * 1024

RNN_COLS = 256
GATE_BW = 128
TIME_CHUNK = 32
PROJ_ROWS = 512
ATTN_TQ = 256


def _rms(x, g):
    ms = jnp.mean(x * x, axis=-1, keepdims=True)
    return x * lax.rsqrt(ms + EPS) * g


def _sigmoid(x):
    return 1.0 / (1.0 + jnp.exp(-x))


def _rglru_kernel(x_ref, norm_ref, winx_ref, wing_ref, convw_ref, convb_ref, wg_ref,
                  brg_ref, big_ref, lam_ref, wout_ref, o_ref,
                  h_scr, xbuf, y_scr, hist, hstate, *, batch, tchunk, ncol):
    rows = batch * tchunk
    halo = (CONV_WIDTH - 1) * batch

    @pl.when(pl.program_id(0) == 0)
    def _():
        hist[...] = jnp.zeros_like(hist)
        hstate[...] = jnp.zeros_like(hstate)

    x = x_ref[...]
    h_scr[...] = _rms(x, norm_ref[...]).astype(BF16)

    def col_block(j, carry):
        hb = h_scr[...]
        xb = jnp.dot(hb, winx_ref[j], preferred_element_type=F32)
        xbuf[0:halo, :] = hist[j]
        xbuf[halo:halo + rows, :] = xb
        hist[j] = xb[rows - halo:, :]
        cw = convw_ref[j]
        xc = convb_ref[j] + cw[3:4, :] * xb
        for k in range(CONV_WIDTH - 1):
            xc = xc + cw[k:k + 1, :] * xbuf[k * batch:k * batch + rows, :]
        xcb = xc.astype(BF16)
        g0 = jnp.dot(xcb[:, :GATE_BW], wg_ref[2 * j], preferred_element_type=F32)
        g1 = jnp.dot(xcb[:, GATE_BW:], wg_ref[2 * j + 1], preferred_element_type=F32)
        r_pre = jnp.concatenate([g0[:, :GATE_BW], g1[:, :GATE_BW]], axis=1) + brg_ref[j]
        i_pre = jnp.concatenate([g0[:, GATE_BW:], g1[:, GATE_BW:]], axis=1) + big_ref[j]
        r = _sigmoid(r_pre)
        ig = _sigmoid(i_pre)
        z = -lam_ref[j]
        softplus = jnp.maximum(z, 0.0) + jnp.log1p(jnp.exp(-jnp.abs(z)))
        log_a = r * (-LRU_C * softplus)
        a = jnp.exp(log_a)
        bt = jnp.sqrt(1.0 - a * a) * (ig * xc)
        hst = hstate[j]
        hs = []
        for t in range(tchunk):
            sl = slice(t * batch, (t + 1) * batch)
            hst = a[sl, :] * hst + bt[sl, :]
            hs.append(hst)
        hstate[j] = hst
        hs = jnp.concatenate(hs, axis=0)
        gate = jnp.dot(hb, wing_ref[j], preferred_element_type=F32)
        y_scr[j] = (hs * (gate * _sigmoid(gate))).astype(BF16)
        return carry

    lax.fori_loop(0, ncol, col_block, 0)

    acc = x
    for j in range(ncol):
        acc = acc + jnp.dot(y_scr[j], wout_ref[j], preferred_element_type=F32)
    o_ref[...] = acc


def _rglru_layer(x_tb, norm_g, w_in, conv_w, conv_b, w_rg, b_rg, w_ig, b_ig, lam, w_out,
                 *, batch):
    n_rows, d_model = x_tb.shape
    d_rnn = w_out.shape[0]
    n_blocks = w_rg.shape[0]
    assert d_rnn % RNN_COLS == 0 and w_rg.shape[1] == GATE_BW
    ncol = d_rnn // RNN_COLS
    rows = batch * TIME_CHUNK
    assert n_rows % rows == 0
    halo = (CONV_WIDTH - 1) * batch

    def cols(w):
        lead = w.shape[:-1]
        w = w.reshape(lead + (ncol, RNN_COLS))
        return jnp.moveaxis(w, -2, 0)

    winx = cols(w_in[:, :d_rnn]).astype(BF16)
    wing = cols(w_in[:, d_rnn:]).astype(BF16)
    convw = cols(conv_w)
    convb = cols(conv_b[None, :])
    wg = jnp.concatenate([w_rg, w_ig], axis=-1).astype(BF16)
    brg = cols(b_rg[None, :])
    big = cols(b_ig[None, :])
    lamc = cols(lam[None, :])
    wout = w_out.reshape(ncol, RNN_COLS, d_model).astype(BF16)

    full = lambda a: pl.BlockSpec(a.shape, lambda i: (0,) * a.ndim)
    kern = functools.partial(_rglru_kernel, batch=batch, tchunk=TIME_CHUNK, ncol=ncol)
    return pl.pallas_call(
        kern,
        out_shape=jax.ShapeDtypeStruct((n_rows, d_model), F32),
        grid=(n_rows // rows,),
        in_specs=[
            pl.BlockSpec((rows, d_model), lambda i: (i, 0)),
            full(norm_g), full(winx), full(wing), full(convw), full(convb), full(wg),
            full(brg), full(big), full(lamc), full(wout),
        ],
        out_specs=pl.BlockSpec((rows, d_model), lambda i: (i, 0)),
        scratch_shapes=[
            pltpu.VMEM((rows, d_model), BF16),
            pltpu.VMEM((halo + rows, RNN_COLS), F32),
            pltpu.VMEM((ncol, rows, RNN_COLS), BF16),
            pltpu.VMEM((ncol, halo, RNN_COLS), F32),
            pltpu.VMEM((ncol, batch, RNN_COLS), F32),
        ],
        compiler_params=pltpu.CompilerParams(
            dimension_semantics=("arbitrary",), vmem_limit_bytes=VMEM_LIMIT_BYTES),
        name="rglru_layer",
    )(x_tb, norm_g, winx, wing, convw, convb, wg, brg, big, lamc, wout)


def _proj_kernel(x_ref, cs_ref, nkv_ref, nb_ref, wdkv_ref, kvn_ref, wuk_ref, wuv_ref,
                 winq_ref, wing_ref, qn_ref, wuq_ref,
                 q_ref, k_ref, v_ref, sg_ref, *, n_heads, kv_rank, nope, rope, v_dim):
    x = x_ref[...]
    ms = jnp.mean(x * x, axis=-1, keepdims=True)
    xn = x * lax.rsqrt(ms + EPS)
    cs = cs_ref[...]

    def roped(tile):
        p = tile * cs
        return (p + pltpu.roll(p, rope, axis=1))[:, :rope]

    hk = (xn * nkv_ref[...]).astype(BF16)
    ckr = jnp.dot(hk, wdkv_ref[...], preferred_element_type=F32)
    c_kv = _rms(ckr[:, :kv_rank], kvn_ref[...]).astype(BF16)
    k_rope = roped(ckr[:, kv_rank:]).astype(BF16)
    k_nope = jnp.dot(c_kv, wuk_ref[...], preferred_element_type=F32)
    v = jnp.dot(c_kv, wuv_ref[...], preferred_element_type=F32)
    for h in range(n_heads):
        k_ref[h, :, 0:nope] = k_nope[:, h * nope:(h + 1) * nope].astype(BF16)
        k_ref[h, :, nope:nope + rope] = k_rope
        v_ref[h] = v[:, h * v_dim:(h + 1) * v_dim].astype(BF16)

    hq = (xn * nb_ref[...]).astype(BF16)
    cq = jnp.dot(hq, winq_ref[...], preferred_element_type=F32)
    c_q = _rms(cq, qn_ref[...]).astype(BF16)
    gate = jnp.dot(hq, wing_ref[...], preferred_element_type=F32)
    sg_ref[...] = (gate * _sigmoid(gate)).astype(BF16)
    qw = nope + 2 * rope
    for h in range(n_heads):
        qh = jnp.dot(c_q, wuq_ref[:, h * qw:(h + 1) * qw], preferred_element_type=F32)
        q_ref[h, :, 0:nope] = qh[:, :nope].astype(BF16)
        q_ref[h, :, nope:nope + rope] = roped(qh[:, nope:]).astype(BF16)


def _swap_halves(w):
    half = w.shape[-1] // 2
    return jnp.concatenate([w[..., half:], w[..., :half]], axis=-1)


def _mla_proj(x1, cs, norm_kv, w_dkv, kv_norm, w_uk, w_uv, norm_b, w_in, q_norm, w_uq):
    bsz, seq, d_model = x1.shape
    kv_rank = kv_norm.shape[0]
    rope = w_dkv.shape[1] - kv_rank
    _, n_heads, nope = w_uk.shape
    v_dim = w_uv.shape[2]
    q_rank = q_norm.shape[0]
    assert 2 * rope == 128 and nope % 128 == 0 and seq % PROJ_ROWS == 0

    wr = w_dkv[:, kv_rank:]
    wdkv = jnp.concatenate([w_dkv[:, :kv_rank], wr, _swap_halves(wr)], axis=1).astype(BF16)
    wuk = w_uk.reshape(kv_rank, n_heads * nope).astype(BF16)
    wuv = w_uv.reshape(kv_rank, n_heads * v_dim).astype(BF16)
    winq = w_in[:, :q_rank].astype(BF16)
    wing = w_in[:, q_rank:].astype(BF16)
    wqr = w_uq[:, :, nope:]
    wuq = jnp.concatenate([w_uq[:, :, :nope], wqr, _swap_halves(wqr)], axis=-1)
    wuq = wuq.reshape(q_rank, n_heads * (nope + 2 * rope)).astype(BF16)

    r2 = lambda a: a.reshape(1, -1)
    args = (x1, cs, r2(norm_kv), r2(norm_b), wdkv, r2(kv_norm), wuk, wuv, winq, wing,
            r2(q_norm), wuq)
    full = lambda a: pl.BlockSpec(a.shape, lambda b, i: (0,) * a.ndim)
    qk = nope + rope
    kern = functools.partial(_proj_kernel, n_heads=n_heads, kv_rank=kv_rank, nope=nope,
                             rope=rope, v_dim=v_dim)
    head_out = lambda d: pl.BlockSpec((None, n_heads, PROJ_ROWS, d), lambda b, i: (b, 0, i, 0))
    return pl.pallas_call(
        kern,
        out_shape=(
            jax.ShapeDtypeStruct((bsz, n_heads, seq, qk), BF16),
            jax.ShapeDtypeStruct((bsz, n_heads, seq, qk), BF16),
            jax.ShapeDtypeStruct((bsz, n_heads, seq, v_dim), BF16),
            jax.ShapeDtypeStruct((bsz, seq, n_heads * v_dim), BF16),
        ),
        grid=(bsz, seq // PROJ_ROWS),
        in_specs=[
            pl.BlockSpec((None, PROJ_ROWS, d_model), lambda b, i: (b, i, 0)),
            pl.BlockSpec((PROJ_ROWS, 4 * (rope // 2)), lambda b, i: (i, 0)),
        ] + [full(a) for a in args[2:]],
        out_specs=(
            head_out(qk), head_out(qk), head_out(v_dim),
            pl.BlockSpec((None, PROJ_ROWS, n_heads * v_dim), lambda b, i: (b, i, 0)),
        ),
        compiler_params=pltpu.CompilerParams(
            dimension_semantics=("parallel", "parallel"), vmem_limit_bytes=VMEM_LIMIT_BYTES),
        name="mla_proj",
    )(*args)


def _attn_kernel(q_ref, k_ref, v_ref, sg_ref, x_ref, wout_ref, fn_ref, o_ref,
                 m_scr, l_scr, acc_scr, *, n_heads, tq, exp2_scale):
    qi = pl.program_id(1)
    reps = tq // LANES
    m_scr[...] = jnp.full(m_scr.shape, NEG_BIG, F32)
    l_scr[...] = jnp.zeros(l_scr.shape, F32)
    acc_scr[...] = jnp.zeros(acc_scr.shape, F32)

    def tile(j, masked):
        start = pl.multiple_of(j * tq, tq)
        if masked:
            row = lax.broadcasted_iota(jnp.int32, (tq, tq), 0)
            col = lax.broadcasted_iota(jnp.int32, (tq, tq), 1)
            causal = col <= row
        for h in range(n_heads):
            k = k_ref[h, pl.ds(start, tq), :]
            v = v_ref[h, pl.ds(start, tq), :]
            s = lax.dot_general(q_ref[h], k, (((1,), (1,)), ((), ())),
                                preferred_element_type=F32)
            if masked:
                s = jnp.where(causal, s, NEG_BIG)
            m_prev = m_scr[h]
            m_new = jnp.maximum(m_prev, jnp.max(s, axis=-1, keepdims=True))
            alpha = jnp.exp2((m_prev - m_new) * exp2_scale)
            p = jnp.exp2((s - jnp.concatenate([m_new] * reps, axis=1)) * exp2_scale)
            l_scr[h] = alpha * l_scr[h] + jnp.sum(p, axis=-1, keepdims=True)
            acc_scr[h] = alpha * acc_scr[h] + jnp.dot(p.astype(BF16), v,
                                                      preferred_element_type=F32)
            m_scr[h] = m_new

    def full_tile(j, carry):
        tile(j, masked=False)
        return carry

    lax.fori_loop(0, qi, full_tile, 0)
    tile(qi, masked=True)

    o = jnp.concatenate([acc_scr[h] / l_scr[h] for h in range(n_heads)], axis=1)
    y = (o * sg_ref[...].astype(F32)).astype(BF16)
    out = x_ref[...] + jnp.dot(y, wout_ref[...], preferred_element_type=F32)
    o_ref[...] = _rms(out, fn_ref[...])


def _mla_attn(q, k, v, sg, x1, w_out, final_norm, *, scale):
    bsz, n_heads, seq, qk = q.shape
    v_dim = v.shape[-1]
    d_model = x1.shape[-1]
    tq = ATTN_TQ
    assert seq % tq == 0
    assert v_dim == LANES and tq % LANES == 0
    kern = functools.partial(_attn_kernel, n_heads=n_heads, tq=tq,
                             exp2_scale=scale * LOG2_E)
    wout = w_out.astype(BF16)
    fn = final_norm.reshape(1, -1)
    return pl.pallas_call(
        kern,
        out_shape=jax.ShapeDtypeStruct((bsz, seq, d_model), F32),
        grid=(bsz, seq // tq),
        in_specs=[
            pl.BlockSpec((None, n_heads, tq, qk), lambda b, i: (b, 0, i, 0)),
            pl.BlockSpec((None, n_heads, seq, qk), lambda b, i: (b, 0, 0, 0)),
            pl.BlockSpec((None, n_heads, seq, v_dim), lambda b, i: (b, 0, 0, 0)),
            pl.BlockSpec((None, tq, n_heads * v_dim), lambda b, i: (b, i, 0)),
            pl.BlockSpec((None, tq, d_model), lambda b, i: (b, i, 0)),
            pl.BlockSpec(wout.shape, lambda b, i: (0, 0)),
            pl.BlockSpec(fn.shape, lambda b, i: (0, 0)),
        ],
        out_specs=pl.BlockSpec((None, tq, d_model), lambda b, i: (b, i, 0)),
        scratch_shapes=[
            pltpu.VMEM((n_heads, tq, LANES), F32),
            pltpu.VMEM((n_heads, tq, LANES), F32),
            pltpu.VMEM((n_heads, tq, v_dim), F32),
        ],
        compiler_params=pltpu.CompilerParams(
            dimension_semantics=("parallel", "arbitrary"), vmem_limit_bytes=VMEM_LIMIT_BYTES),
        name="mla_attn",
    )(q, k, v, sg, x1, wout, fn)


def _rope_table(seq, rope):
    pos = jnp.arange(seq, dtype=F32)
    inv = ROPE_THETA ** (-jnp.arange(0, rope, 2, dtype=F32) / rope)
    ang = pos[:, None] * inv[None, :]
    cos, sin = jnp.cos(ang), jnp.sin(ang)
    return jnp.concatenate([cos, cos, -sin, sin], axis=1)


def kernel(x, norm_a, w_in_a, conv_w, conv_b, w_rg, b_rg, w_ig, b_ig, lru_lambda, w_out_a,
           norm_kv, w_dkv, kv_norm, w_uk, w_uv, norm_b, w_in_b, q_norm, w_uq, w_out_b,
           final_norm):
    bsz, seq, d_model = x.shape
    n_a = w_in_a.shape[0]
    n_b = w_in_b.shape[0]
    assert n_b == 1, "the final norm is fused into the (single) MLA layer"
    nope = w_uk.shape[-1]
    rope = w_dkv.shape[1] - kv_norm.shape[0]
    scale = float(nope + rope) ** -0.5

    x_tb = jnp.transpose(x, (1, 0, 2)).reshape(seq * bsz, d_model)
    for layer in range(n_a):
        x_tb = _rglru_layer(x_tb, norm_a[layer].reshape(1, -1), w_in_a[layer], conv_w[layer],
                            conv_b[layer], w_rg[layer], b_rg[layer], w_ig[layer], b_ig[layer],
                            lru_lambda[layer], w_out_a[layer], batch=bsz)
    x1 = jnp.transpose(x_tb.reshape(seq, bsz, d_model), (1, 0, 2))

    cs = _rope_table(seq, rope)
    q, k, v, sg = _mla_proj(x1, cs, norm_kv, w_dkv, kv_norm, w_uk, w_uv, norm_b[0],
                            w_in_b[0], q_norm[0], w_uq[0])
    return _mla_attn(q, k, v, sg, x1, w_out_b[0], final_norm, scale=scale)
```

```python
import functools

import jax
import jax.numpy as jnp
from jax import lax
from jax.experimental import pallas as pl
from jax.experimental.pallas import tpu as pltpu

F32 = jnp.float32
BF16 = jnp.bfloat16

EPS = 1e-6
LRU_C = 8.0
ROPE_THETA = 10000.0
CONV_WIDTH = 4
NEG_BIG = -1e30
LOG2_E = 1.4426950408889634
LANES = 128

VMEM_LIMIT_BYTES = 56 * 1024 * 1024

RNN_COLS = 256
GATE_BW = 128
TIME_CHUNK = 32
PROJ_ROWS = 512
ATTN_TQ = 512


def _rms(x, g):
    ms = jnp.mean(x * x, axis=-1, keepdims=True)
    return x * lax.rsqrt(ms + EPS) * g


def _sigmoid(x):
    return 1.0 / (1.0 + jnp.exp(-x))


def _rglru_kernel(x_ref, norm_ref, winx_ref, wing_ref, convw_ref, convb_ref, wg_ref,
                  brg_ref, big_ref, lam_ref, wout_ref, o_ref,
                  h_scr, xbuf, y_scr, hist, hstate, *, batch, tchunk, ncol):
    rows = batch * tchunk
    halo = (CONV_WIDTH - 1) * batch

    @pl.when(pl.program_id(0) == 0)
    def _():
        hist[...] = jnp.zeros_like(hist)
        hstate[...] = jnp.zeros_like(hstate)

    x = x_ref[...]
    h_scr[...] = _rms(x, norm_ref[...]).astype(BF16)

    def col_block(j, carry):
        hb = h_scr[...]
        xb = jnp.dot(hb, winx_ref[j], preferred_element_type=F32)
        xbuf[0:halo, :] = hist[j]
        xbuf[halo:halo + rows, :] = xb
        hist[j] = xb[rows - halo:, :]
        cw = convw_ref[j]
        xc = convb_ref[j] + cw[3:4, :] * xb
        for k in range(CONV_WIDTH - 1):
            xc = xc + cw[k:k + 1, :] * xbuf[k * batch:k * batch + rows, :]
        xcb = xc.astype(BF16)
        g0 = jnp.dot(xcb[:, :GATE_BW], wg_ref[2 * j], preferred_element_type=F32)
        g1 = jnp.dot(xcb[:, GATE_BW:], wg_ref[2 * j + 1], preferred_element_type=F32)
        r_pre = jnp.concatenate([g0[:, :GATE_BW], g1[:, :GATE_BW]], axis=1) + brg_ref[j]
        i_pre = jnp.concatenate([g0[:, GATE_BW:], g1[:, GATE_BW:]], axis=1) + big_ref[j]
        r = _sigmoid(r_pre)
        ig = _sigmoid(i_pre)
        z = -lam_ref[j]
        softplus = jnp.maximum(z, 0.0) + jnp.log1p(jnp.exp(-jnp.abs(z)))
        log_a = r * (-LRU_C * softplus)
        a = jnp.exp(log_a)
        bt = jnp.sqrt(1.0 - a * a) * (ig * xc)
        hst = hstate[j]
        hs = []
        for t in range(tchunk):
            sl = slice(t * batch, (t + 1) * batch)
            hst = a[sl, :] * hst + bt[sl, :]
            hs.append(hst)
        hstate[j] = hst
        hs = jnp.concatenate(hs, axis=0)
        gate = jnp.dot(hb, wing_ref[j], preferred_element_type=F32)
        y_scr[j] = (hs * (gate * _sigmoid(gate))).astype(BF16)
        return carry

    lax.fori_loop(0, ncol, col_block, 0)

    acc = x
    for j in range(ncol):
        acc = acc + jnp.dot(y_scr[j], wout_ref[j], preferred_element_type=F32)
    o_ref[...] = acc


def _rglru_layer(x_tb, norm_g, w_in, conv_w, conv_b, w_rg, b_rg, w_ig, b_ig, lam, w_out,
                 *, batch):
    n_rows, d_model = x_tb.shape
    d_rnn = w_out.shape[0]
    n_blocks = w_rg.shape[0]
    assert d_rnn % RNN_COLS == 0 and w_rg.shape[1] == GATE_BW
    ncol = d_rnn // RNN_COLS
    rows = batch * TIME_CHUNK
    assert n_rows % rows == 0
    halo = (CONV_WIDTH - 1) * batch

    def cols(w):
        lead = w.shape[:-1]
        w = w.reshape(lead + (ncol, RNN_COLS))
        return jnp.moveaxis(w, -2, 0)

    winx = cols(w_in[:, :d_rnn]).astype(BF16)
    wing = cols(w_in[:, d_rnn:]).astype(BF16)
    convw = cols(conv_w)
    convb = cols(conv_b[None, :])
    wg = jnp.concatenate([w_rg, w_ig], axis=-1).astype(BF16)
    brg = cols(b_rg[None, :])
    big = cols(b_ig[None, :])
    lamc = cols(lam[None, :])
    wout = w_out.reshape(ncol, RNN_COLS, d_model).astype(BF16)

    full = lambda a: pl.BlockSpec(a.shape, lambda i: (0,) * a.ndim)
    kern = functools.partial(_rglru_kernel, batch=batch, tchunk=TIME_CHUNK, ncol=ncol)
    return pl.pallas_call(
        kern,
        out_shape=jax.ShapeDtypeStruct((n_rows, d_model), F32),
        grid=(n_rows // rows,),
        in_specs=[
            pl.BlockSpec((rows, d_model), lambda i: (i, 0)),
            full(norm_g), full(winx), full(wing), full(convw), full(convb), full(wg),
            full(brg), full(big), full(lamc), full(wout),
        ],
        out_specs=pl.BlockSpec((rows, d_model), lambda i: (i, 0)),
        scratch_shapes=[
            pltpu.VMEM((rows, d_model), BF16),
            pltpu.VMEM((halo + rows, RNN_COLS), F32),
            pltpu.VMEM((ncol, rows, RNN_COLS), BF16),
            pltpu.VMEM((ncol, halo, RNN_COLS), F32),
            pltpu.VMEM((ncol, batch, RNN_COLS), F32),
        ],
        compiler_params=pltpu.CompilerParams(
            dimension_semantics=("arbitrary",), vmem_limit_bytes=VMEM_LIMIT_BYTES),
        name="rglru_layer",
    )(x_tb, norm_g, winx, wing, convw, convb, wg, brg, big, lamc, wout)


def _proj_kernel(x_ref, cs_ref, nkv_ref, nb_ref, wdkv_ref, kvn_ref, wuk_ref, wuv_ref,
                 winq_ref, wing_ref, qn_ref, wuq_ref,
                 q_ref, k_ref, v_ref, sg_ref, *, n_heads, kv_rank, nope, rope, v_dim):
    x = x_ref[...]
    ms = jnp.mean(x * x, axis=-1, keepdims=True)
    xn = x * lax.rsqrt(ms + EPS)
    cs = cs_ref[...]

    def roped(tile):
        p = tile * cs
        return (p + pltpu.roll(p, rope, axis=1))[:, :rope]

    hk = (xn * nkv_ref[...]).astype(BF16)
    ckr = jnp.dot(hk, wdkv_ref[...], preferred_element_type=F32)
    c_kv = _rms(ckr[:, :kv_rank], kvn_ref[...]).astype(BF16)
    k_rope = roped(ckr[:, kv_rank:]).astype(BF16)
    k_nope = jnp.dot(c_kv, wuk_ref[...], preferred_element_type=F32)
    v = jnp.dot(c_kv, wuv_ref[...], preferred_element_type=F32)
    for h in range(n_heads):
        k_ref[h, :, 0:nope] = k_nope[:, h * nope:(h + 1) * nope].astype(BF16)
        k_ref[h, :, nope:nope + rope] = k_rope
        v_ref[h] = v[:, h * v_dim:(h + 1) * v_dim].astype(BF16)

    hq = (xn * nb_ref[...]).astype(BF16)
    cq = jnp.dot(hq, winq_ref[...], preferred_element_type=F32)
    c_q = _rms(cq, qn_ref[...]).astype(BF16)
    gate = jnp.dot(hq, wing_ref[...], preferred_element_type=F32)
    sg_ref[...] = (gate * _sigmoid(gate)).astype(BF16)
    qw = nope + 2 * rope
    for h in range(n_heads):
        qh = jnp.dot(c_q, wuq_ref[:, h * qw:(h + 1) * qw], preferred_element_type=F32)
        q_ref[h, :, 0:nope] = qh[:, :nope].astype(BF16)
        q_ref[h, :, nope:nope + rope] = roped(qh[:, nope:]).astype(BF16)


def _swap_halves(w):
    half = w.shape[-1] // 2
    return jnp.concatenate([w[..., half:], w[..., :half]], axis=-1)


def _mla_proj(x1, cs, norm_kv, w_dkv, kv_norm, w_uk, w_uv, norm_b, w_in, q_norm, w_uq):
    bsz, seq, d_model = x1.shape
    kv_rank = kv_norm.shape[0]
    rope = w_dkv.shape[1] - kv_rank
    _, n_heads, nope = w_uk.shape
    v_dim = w_uv.shape[2]
    q_rank = q_norm.shape[0]
    assert 2 * rope == 128 and nope % 128 == 0 and seq % PROJ_ROWS == 0

    wr = w_dkv[:, kv_rank:]
    wdkv = jnp.concatenate([w_dkv[:, :kv_rank], wr, _swap_halves(wr)], axis=1).astype(BF16)
    wuk = w_uk.reshape(kv_rank, n_heads * nope).astype(BF16)
    wuv = w_uv.reshape(kv_rank, n_heads * v_dim).astype(BF16)
    winq = w_in[:, :q_rank].astype(BF16)
    wing = w_in[:, q_rank:].astype(BF16)
    wqr = w_uq[:, :, nope:]
    wuq = jnp.concatenate([w_uq[:, :, :nope], wqr, _swap_halves(wqr)], axis=-1)
    wuq = wuq.reshape(q_rank, n_heads * (nope + 2 * rope)).astype(BF16)

    r2 = lambda a: a.reshape(1, -1)
    args = (x1, cs, r2(norm_kv), r2(norm_b), wdkv, r2(kv_norm), wuk, wuv, winq, wing,
            r2(q_norm), wuq)
    full = lambda a: pl.BlockSpec(a.shape, lambda b, i: (0,) * a.ndim)
    qk = nope + rope
    kern = functools.partial(_proj_kernel, n_heads=n_heads, kv_rank=kv_rank, nope=nope,
                             rope=rope, v_dim=v_dim)
    head_out = lambda d: pl.BlockSpec((None, n_heads, PROJ_ROWS, d), lambda b, i: (b, 0, i, 0))
    return pl.pallas_call(
        kern,
        out_shape=(
            jax.ShapeDtypeStruct((bsz, n_heads, seq, qk), BF16),
            jax.ShapeDtypeStruct((bsz, n_heads, seq, qk), BF16),
            jax.ShapeDtypeStruct((bsz, n_heads, seq, v_dim), BF16),
            jax.ShapeDtypeStruct((bsz, seq, n_heads * v_dim), BF16),
        ),
        grid=(bsz, seq // PROJ_ROWS),
        in_specs=[
            pl.BlockSpec((None, PROJ_ROWS, d_model), lambda b, i: (b, i, 0)),
            pl.BlockSpec((PROJ_ROWS, 4 * (rope // 2)), lambda b, i: (i, 0)),
        ] + [full(a) for a in args[2:]],
        out_specs=(
            head_out(qk), head_out(qk), head_out(v_dim),
            pl.BlockSpec((None, PROJ_ROWS, n_heads * v_dim), lambda b, i: (b, i, 0)),
        ),
        compiler_params=pltpu.CompilerParams(
            dimension_semantics=("parallel", "parallel"), vmem_limit_bytes=VMEM_LIMIT_BYTES),
        name="mla_proj",
    )(*args)


def _attn_kernel(q_ref, k_ref, v_ref, sg_ref, x_ref, wout_ref, fn_ref, o_ref,
                 m_scr, acc_scr, *, n_heads, tq, exp2_scale):
    qi = pl.program_id(1)
    half = tq // 2
    m_scr[...] = jnp.full(m_scr.shape, NEG_BIG, F32)
    acc_scr[...] = jnp.zeros(acc_scr.shape, F32)

    def attend(r0, nrows, kstart, width, diag_off):
        if diag_off is not None:
            row = lax.broadcasted_iota(jnp.int32, (nrows, width), 0)
            col = lax.broadcasted_iota(jnp.int32, (nrows, width), 1)
            keep = col <= row + diag_off
        ones = jnp.ones((width, LANES), BF16)
        for h in range(n_heads):
            q = q_ref[h, r0:r0 + nrows, :]
            k = k_ref[h, pl.ds(kstart, width), :]
            v = jnp.concatenate([v_ref[h, pl.ds(kstart, width), :], ones], axis=1)
            s = lax.dot_general(q, k, (((1,), (1,)), ((), ())),
                                preferred_element_type=F32)
            if diag_off is not None:
                s = jnp.where(keep, s, NEG_BIG)
            m_prev = m_scr[h, r0:r0 + nrows, :]
            m_new = jnp.maximum(m_prev, jnp.max(s, axis=-1, keepdims=True))
            alpha = jnp.exp2((m_prev - m_new) * exp2_scale)
            p = jnp.exp2((s - jnp.concatenate([m_new] * (width // LANES), axis=1)) * exp2_scale)
            pv = jnp.dot(p.astype(BF16), v, preferred_element_type=F32)
            acc_scr[h, r0:r0 + nrows, :] = (
                jnp.concatenate([alpha, alpha], axis=1) * acc_scr[h, r0:r0 + nrows, :] + pv)
            m_scr[h, r0:r0 + nrows, :] = m_new

    def full_tile(j, carry):
        attend(0, tq, pl.multiple_of(j * tq, tq), tq, None)
        return carry

    lax.fori_loop(0, qi, full_tile, 0)
    diag = pl.multiple_of(qi * tq, tq)
    attend(0, half, diag, half, 0)
    attend(half, half, diag, tq, half)

    o = jnp.concatenate([acc_scr[h, :, :LANES] / acc_scr[h, :, LANES:] for h in range(n_heads)],
                        axis=1)
    y = (o * sg_ref[...].astype(F32)).astype(BF16)
    out = x_ref[...] + jnp.dot(y, wout_ref[...], preferred_element_type=F32)
    o_ref[...] = _rms(out, fn_ref[...])


def _mla_attn(q, k, v, sg, x1, w_out, final_norm, *, scale):
    bsz, n_heads, seq, qk = q.shape
    v_dim = v.shape[-1]
    d_model = x1.shape[-1]
    tq = ATTN_TQ
    assert seq % tq == 0
    assert v_dim == LANES and tq % LANES == 0
    kern = functools.partial(_attn_kernel, n_heads=n_heads, tq=tq,
                             exp2_scale=scale * LOG2_E)
    wout = w_out.astype(BF16)
    fn = final_norm.reshape(1, -1)
    return pl.pallas_call(
        kern,
        out_shape=jax.ShapeDtypeStruct((bsz, seq, d_model), F32),
        grid=(bsz, seq // tq),
        in_specs=[
            pl.BlockSpec((None, n_heads, tq, qk), lambda b, i: (b, 0, i, 0)),
            pl.BlockSpec((None, n_heads, seq, qk), lambda b, i: (b, 0, 0, 0)),
            pl.BlockSpec((None, n_heads, seq, v_dim), lambda b, i: (b, 0, 0, 0)),
            pl.BlockSpec((None, tq, n_heads * v_dim), lambda b, i: (b, i, 0)),
            pl.BlockSpec((None, tq, d_model), lambda b, i: (b, i, 0)),
            pl.BlockSpec(wout.shape, lambda b, i: (0, 0)),
            pl.BlockSpec(fn.shape, lambda b, i: (0, 0)),
        ],
        out_specs=pl.BlockSpec((None, tq, d_model), lambda b, i: (b, i, 0)),
        scratch_shapes=[
            pltpu.VMEM((n_heads, tq, LANES), F32),
            pltpu.VMEM((n_heads, tq, v_dim + LANES), F32),
        ],
        compiler_params=pltpu.CompilerParams(
            dimension_semantics=("parallel", "arbitrary"), vmem_limit_bytes=VMEM_LIMIT_BYTES),
        name="mla_attn",
    )(q, k, v, sg, x1, wout, fn)


def _rope_table(seq, rope):
    pos = jnp.arange(seq, dtype=F32)
    inv = ROPE_THETA ** (-jnp.arange(0, rope, 2, dtype=F32) / rope)
    ang = pos[:, None] * inv[None, :]
    cos, sin = jnp.cos(ang), jnp.sin(ang)
    return jnp.concatenate([cos, cos, -sin, sin], axis=1)


def kernel(x, norm_a, w_in_a, conv_w, conv_b, w_rg, b_rg, w_ig, b_ig, lru_lambda, w_out_a,
           norm_kv, w_dkv, kv_norm, w_uk, w_uv, norm_b, w_in_b, q_norm, w_uq, w_out_b,
           final_norm):
    bsz, seq, d_model = x.shape
    n_a = w_in_a.shape[0]
    n_b = w_in_b.shape[0]
    assert n_b == 1, "the final norm is fused into the (single) MLA layer"
    nope = w_uk.shape[-1]
    rope = w_dkv.shape[1] - kv_norm.shape[0]
    scale = float(nope + rope) ** -0.5

    x_tb = jnp.transpose(x, (1, 0, 2)).reshape(seq * bsz, d_model)
    for layer in range(n_a):
        x_tb = _rglru_layer(x_tb, norm_a[layer].reshape(1, -1), w_in_a[layer], conv_w[layer],
                            conv_b[layer], w_rg[layer], b_rg[layer], w_ig[layer], b_ig[layer],
                            lru_lambda[layer], w_out_a[layer], batch=bsz)
    x1 = jnp.transpose(x_tb.reshape(seq, bsz, d_model), (1, 0, 2))

    cs = _rope_table(seq, rope)
    q, k, v, sg = _mla_proj(x1, cs, norm_kv, w_dkv, kv_norm, w_uk, w_uv, norm_b[0],
                            w_in_b[0], q_norm[0], w_uq[0])
    return _mla_attn(q, k, v, sg, x1, w_out_b[0], final_norm, scale=scale)
```

```python
import functools

import jax
import jax.numpy as jnp
from jax import lax
from jax.experimental import pallas as pl
from jax.experimental.pallas import tpu as pltpu

F32 = jnp.float32
BF16 = jnp.bfloat16

EPS = 1e-6
LRU_C = 8.0
ROPE_THETA = 10000.0
CONV_WIDTH = 4
NEG_BIG = -1e30
LOG2_E = 1.4426950408889634
LANES = 128

VMEM_LIMIT_BYTES = 56 * 1024 * 1024

RNN_COLS = 256
GATE_BW = 128
TIME_CHUNK = 32
PROJ_ROWS = 512
ATTN_TQ = 512


def _rms(x, g):
    ms = jnp.mean(x * x, axis=-1, keepdims=True)
    return x * lax.rsqrt(ms + EPS) * g


def _sigmoid(x):
    return 1.0 / (1.0 + jnp.exp2(x * (-LOG2_E)))


def _sqrt_unit(t):
    return jnp.where(t > 0.0, t * lax.rsqrt(t), 0.0)


def _rglru_kernel(x_hbm, norm_ref, winx_ref, wing_ref, convw_ref, convb_ref, wg_ref,
                  brg_ref, big_ref, lam_ref, wout_ref, o_hbm,
                  xin, xout, in_sem, out_sem, h_scr, xbuf, y_scr, hist, hstate,
                  *, batch, tchunk, ncol, nsteps):
    rows = batch * tchunk
    halo = (CONV_WIDTH - 1) * batch
    step = pl.program_id(0)
    slot = lax.rem(step, 2)

    def in_copies(s, sl):
        return [pltpu.make_async_copy(x_hbm.at[b, pl.ds(s * tchunk, tchunk), :],
                                      xin.at[sl, :, b, :], in_sem.at[sl]) for b in range(batch)]

    def out_copies(s, sl):
        return [pltpu.make_async_copy(xout.at[sl, :, b, :],
                                      o_hbm.at[b, pl.ds(s * tchunk, tchunk), :], out_sem.at[sl])
                for b in range(batch)]

    @pl.when(step == 0)
    def _():
        hist[...] = jnp.zeros_like(hist)
        hstate[...] = jnp.zeros_like(hstate)
        for c in in_copies(0, 0):
            c.start()

    @pl.when(step + 1 < nsteps)
    def _():
        for c in in_copies(step + 1, 1 - slot):
            c.start()

    for c in in_copies(step, slot):
        c.wait()

    x = xin[slot].reshape(rows, xin.shape[-1])
    h_scr[...] = _rms(x, norm_ref[...]).astype(BF16)

    def col_block(j):
        hb = h_scr[...]
        xb = jnp.dot(hb, winx_ref[j], preferred_element_type=F32)
        xbuf[0:halo, :] = hist[j]
        xbuf[halo:halo + rows, :] = xb
        hist[j] = xb[rows - halo:, :]
        cw = convw_ref[j]
        xc = convb_ref[j] + cw[3:4, :] * xb
        for k in range(CONV_WIDTH - 1):
            xc = xc + cw[k:k + 1, :] * xbuf[k * batch:k * batch + rows, :]
        xcb = xc.astype(BF16)
        g0 = jnp.dot(xcb[:, :GATE_BW], wg_ref[2 * j], preferred_element_type=F32)
        g1 = jnp.dot(xcb[:, GATE_BW:], wg_ref[2 * j + 1], preferred_element_type=F32)
        r_pre = jnp.concatenate([g0[:, :GATE_BW], g1[:, :GATE_BW]], axis=1) + brg_ref[j]
        i_pre = jnp.concatenate([g0[:, GATE_BW:], g1[:, GATE_BW:]], axis=1) + big_ref[j]
        r = _sigmoid(r_pre)
        ig = _sigmoid(i_pre)
        z = -lam_ref[j]
        softplus = jnp.maximum(z, 0.0) + jnp.log1p(jnp.exp(-jnp.abs(z)))
        log_a = r * (-LRU_C * softplus)
        a = jnp.exp(log_a)
        bt = _sqrt_unit(1.0 - a * a) * (ig * xc)
        hst = hstate[j]
        hs = []
        for t in range(tchunk):
            sl = slice(t * batch, (t + 1) * batch)
            hst = a[sl, :] * hst + bt[sl, :]
            hs.append(hst)
        hstate[j] = hst
        hs = jnp.concatenate(hs, axis=0)
        gate = jnp.dot(hb, wing_ref[j], preferred_element_type=F32)
        y_scr[j] = (hs * (gate * _sigmoid(gate))).astype(BF16)

    for j in range(ncol):
        col_block(j)

    acc = x
    for j in range(ncol):
        acc = acc + jnp.dot(y_scr[j], wout_ref[j], preferred_element_type=F32)

    @pl.when(step >= 2)
    def _():
        for c in out_copies(step - 2, slot):
            c.wait()

    xout[slot] = acc.reshape(tchunk, batch, xout.shape[-1])
    for c in out_copies(step, slot):
        c.start()

    @pl.when(step == nsteps - 1)
    def _():
        for c in out_copies(step - 1, 1 - slot):
            c.wait()
        for c in out_copies(step, slot):
            c.wait()


def _rglru_layer(x, norm_g, w_in, conv_w, conv_b, w_rg, b_rg, w_ig, b_ig, lam, w_out):
    batch, seq, d_model = x.shape
    d_rnn = w_out.shape[0]
    assert d_rnn % RNN_COLS == 0 and w_rg.shape[1] == GATE_BW
    ncol = d_rnn // RNN_COLS
    rows = batch * TIME_CHUNK
    nsteps = seq // TIME_CHUNK
    assert seq % TIME_CHUNK == 0 and nsteps >= 2 and batch % 8 == 0
    halo = (CONV_WIDTH - 1) * batch

    def cols(w):
        lead = w.shape[:-1]
        w = w.reshape(lead + (ncol, RNN_COLS))
        return jnp.moveaxis(w, -2, 0)

    winx = cols(w_in[:, :d_rnn]).astype(BF16)
    wing = cols(w_in[:, d_rnn:]).astype(BF16)
    convw = cols(conv_w)
    convb = cols(conv_b[None, :])
    wg = jnp.concatenate([w_rg, w_ig], axis=-1).astype(BF16)
    brg = cols(b_rg[None, :])
    big = cols(b_ig[None, :])
    lamc = cols(lam[None, :])
    wout = w_out.reshape(ncol, RNN_COLS, d_model).astype(BF16)

    full = lambda a: pl.BlockSpec(a.shape, lambda i: (0,) * a.ndim)
    kern = functools.partial(_rglru_kernel, batch=batch, tchunk=TIME_CHUNK, ncol=ncol,
                             nsteps=nsteps)
    return pl.pallas_call(
        kern,
        out_shape=jax.ShapeDtypeStruct((batch, seq, d_model), F32),
        grid=(nsteps,),
        in_specs=[
            pl.BlockSpec(memory_space=pl.ANY),
            full(norm_g), full(winx), full(wing), full(convw), full(convb), full(wg),
            full(brg), full(big), full(lamc), full(wout),
        ],
        out_specs=pl.BlockSpec(memory_space=pl.ANY),
        scratch_shapes=[
            pltpu.VMEM((2, TIME_CHUNK, batch, d_model), F32),
            pltpu.VMEM((2, TIME_CHUNK, batch, d_model), F32),
            pltpu.SemaphoreType.DMA((2,)),
            pltpu.SemaphoreType.DMA((2,)),
            pltpu.VMEM((rows, d_model), BF16),
            pltpu.VMEM((halo + rows, RNN_COLS), F32),
            pltpu.VMEM((ncol, rows, RNN_COLS), BF16),
            pltpu.VMEM((ncol, halo, RNN_COLS), F32),
            pltpu.VMEM((ncol, batch, RNN_COLS), F32),
        ],
        compiler_params=pltpu.CompilerParams(
            dimension_semantics=("arbitrary",), vmem_limit_bytes=VMEM_LIMIT_BYTES),
        name="rglru_layer",
    )(x, norm_g, winx, wing, convw, convb, wg, brg, big, lamc, wout)


def _proj_kernel(x_ref, cs_ref, nkv_ref, nb_ref, wdkv_ref, kvn_ref, wuk_ref, wuv_ref,
                 winq_ref, wing_ref, qn_ref, wuq_ref,
                 q_ref, k_ref, v_ref, sg_ref, *, n_heads, kv_rank, nope, rope, v_dim):
    x = x_ref[...]
    ms = jnp.mean(x * x, axis=-1, keepdims=True)
    xn = x * lax.rsqrt(ms + EPS)
    cs = cs_ref[...]

    def roped(tile):
        p = tile * cs
        return (p + pltpu.roll(p, rope, axis=1))[:, :rope]

    hk = (xn * nkv_ref[...]).astype(BF16)
    ckr = jnp.dot(hk, wdkv_ref[...], preferred_element_type=F32)
    c_kv = _rms(ckr[:, :kv_rank], kvn_ref[...]).astype(BF16)
    k_rope = roped(ckr[:, kv_rank:]).astype(BF16)
    k_nope = jnp.dot(c_kv, wuk_ref[...], preferred_element_type=F32)
    v = jnp.dot(c_kv, wuv_ref[...], preferred_element_type=F32)
    for h in range(n_heads):
        k_ref[h, :, 0:nope] = k_nope[:, h * nope:(h + 1) * nope].astype(BF16)
        k_ref[h, :, nope:nope + rope] = k_rope
        v_ref[h] = v[:, h * v_dim:(h + 1) * v_dim].astype(BF16)

    hq = (xn * nb_ref[...]).astype(BF16)
    cq = jnp.dot(hq, winq_ref[...], preferred_element_type=F32)
    c_q = _rms(cq, qn_ref[...]).astype(BF16)
    gate = jnp.dot(hq, wing_ref[...], preferred_element_type=F32)
    sg_ref[...] = (gate * _sigmoid(gate)).astype(BF16)
    qw = nope + 2 * rope
    for h in range(n_heads):
        qh = jnp.dot(c_q, wuq_ref[:, h * qw:(h + 1) * qw], preferred_element_type=F32)
        q_ref[h, :, 0:nope] = qh[:, :nope].astype(BF16)
        q_ref[h, :, nope:nope + rope] = roped(qh[:, nope:]).astype(BF16)


def _swap_halves(w):
    half = w.shape[-1] // 2
    return jnp.concatenate([w[..., half:], w[..., :half]], axis=-1)


def _mla_proj(x1, cs, norm_kv, w_dkv, kv_norm, w_uk, w_uv, norm_b, w_in, q_norm, w_uq):
    bsz, seq, d_model = x1.shape
    kv_rank = kv_norm.shape[0]
    rope = w_dkv.shape[1] - kv_rank
    _, n_heads, nope = w_uk.shape
    v_dim = w_uv.shape[2]
    q_rank = q_norm.shape[0]
    assert 2 * rope == 128 and nope % 128 == 0 and seq % PROJ_ROWS == 0

    wr = w_dkv[:, kv_rank:]
    wdkv = jnp.concatenate([w_dkv[:, :kv_rank], wr, _swap_halves(wr)], axis=1).astype(BF16)
    wuk = w_uk.reshape(kv_rank, n_heads * nope).astype(BF16)
    wuv = w_uv.reshape(kv_rank, n_heads * v_dim).astype(BF16)
    winq = w_in[:, :q_rank].astype(BF16)
    wing = w_in[:, q_rank:].astype(BF16)
    wqr = w_uq[:, :, nope:]
    wuq = jnp.concatenate([w_uq[:, :, :nope], wqr, _swap_halves(wqr)], axis=-1)
    wuq = wuq.reshape(q_rank, n_heads * (nope + 2 * rope)).astype(BF16)

    r2 = lambda a: a.reshape(1, -1)
    args = (x1, cs, r2(norm_kv), r2(norm_b), wdkv, r2(kv_norm), wuk, wuv, winq, wing,
            r2(q_norm), wuq)
    full = lambda a: pl.BlockSpec(a.shape, lambda b, i: (0,) * a.ndim)
    qk = nope + rope
    kern = functools.partial(_proj_kernel, n_heads=n_heads, kv_rank=kv_rank, nope=nope,
                             rope=rope, v_dim=v_dim)
    head_out = lambda d: pl.BlockSpec((None, n_heads, PROJ_ROWS, d), lambda b, i: (b, 0, i, 0))
    return pl.pallas_call(
        kern,
        out_shape=(
            jax.ShapeDtypeStruct((bsz, n_heads, seq, qk), BF16),
            jax.ShapeDtypeStruct((bsz, n_heads, seq, qk), BF16),
            jax.ShapeDtypeStruct((bsz, n_heads, seq, v_dim), BF16),
            jax.ShapeDtypeStruct((bsz, seq, n_heads * v_dim), BF16),
        ),
        grid=(bsz, seq // PROJ_ROWS),
        in_specs=[
            pl.BlockSpec((None, PROJ_ROWS, d_model), lambda b, i: (b, i, 0)),
            pl.BlockSpec((PROJ_ROWS, 4 * (rope // 2)), lambda b, i: (i, 0)),
        ] + [full(a) for a in args[2:]],
        out_specs=(
            head_out(qk), head_out(qk), head_out(v_dim),
            pl.BlockSpec((None, PROJ_ROWS, n_heads * v_dim), lambda b, i: (b, i, 0)),
        ),
        compiler_params=pltpu.CompilerParams(
            dimension_semantics=("parallel", "parallel"), vmem_limit_bytes=VMEM_LIMIT_BYTES),
        name="mla_proj",
    )(*args)


def _attn_kernel(q_ref, k_ref, v_ref, sg_ref, x_ref, wout_ref, fn_ref, o_ref,
                 m_scr, acc_scr, *, n_heads, tq, exp2_scale):
    qi = pl.program_id(1)
    half = tq // 2
    m_scr[...] = jnp.full(m_scr.shape, NEG_BIG, F32)
    acc_scr[...] = jnp.zeros(acc_scr.shape, F32)

    def attend(r0, nrows, kstart, width, diag_off):
        if diag_off is not None:
            row = lax.broadcasted_iota(jnp.int32, (nrows, width), 0)
            col = lax.broadcasted_iota(jnp.int32, (nrows, width), 1)
            keep = col <= row + diag_off
        ones = jnp.ones((width, LANES), BF16)
        for h in range(n_heads):
            q = q_ref[h, r0:r0 + nrows, :]
            k = k_ref[h, pl.ds(kstart, width), :]
            v = jnp.concatenate([v_ref[h, pl.ds(kstart, width), :], ones], axis=1)
            s = lax.dot_general(q, k, (((1,), (1,)), ((), ())),
                                preferred_element_type=F32)
            if diag_off is not None:
                s = jnp.where(keep, s, NEG_BIG)
            m_prev = m_scr[h, r0:r0 + nrows, :]
            m_new = jnp.maximum(m_prev, jnp.max(s, axis=-1, keepdims=True))
            alpha = jnp.exp2((m_prev - m_new) * exp2_scale)
            p = jnp.exp2((s - jnp.concatenate([m_new] * (width // LANES), axis=1)) * exp2_scale)
            pv = jnp.dot(p.astype(BF16), v, preferred_element_type=F32)
            acc_scr[h, r0:r0 + nrows, :] = (
                jnp.concatenate([alpha, alpha], axis=1) * acc_scr[h, r0:r0 + nrows, :] + pv)
            m_scr[h, r0:r0 + nrows, :] = m_new

    def full_tile(j, carry):
        attend(0, tq, pl.multiple_of(j * tq, tq), tq, None)
        return carry

    lax.fori_loop(0, qi, full_tile, 0)
    diag = pl.multiple_of(qi * tq, tq)
    attend(0, half, diag, half, 0)
    attend(half, half, diag, tq, half)

    o = jnp.concatenate([acc_scr[h, :, :LANES] / acc_scr[h, :, LANES:] for h in range(n_heads)],
                        axis=1)
    y = (o * sg_ref[...].astype(F32)).astype(BF16)
    out = x_ref[...] + jnp.dot(y, wout_ref[...], preferred_element_type=F32)
    o_ref[...] = _rms(out, fn_ref[...])


def _mla_attn(q, k, v, sg, x1, w_out, final_norm, *, scale):
    bsz, n_heads, seq, qk = q.shape
    v_dim = v.shape[-1]
    d_model = x1.shape[-1]
    tq = ATTN_TQ
    assert seq % tq == 0
    assert v_dim == LANES and tq % LANES == 0
    kern = functools.partial(_attn_kernel, n_heads=n_heads, tq=tq,
                             exp2_scale=scale * LOG2_E)
    wout = w_out.astype(BF16)
    fn = final_norm.reshape(1, -1)
    return pl.pallas_call(
        kern,
        out_shape=jax.ShapeDtypeStruct((bsz, seq, d_model), F32),
        grid=(bsz, seq // tq),
        in_specs=[
            pl.BlockSpec((None, n_heads, tq, qk), lambda b, i: (b, 0, i, 0)),
            pl.BlockSpec((None, n_heads, seq, qk), lambda b, i: (b, 0, 0, 0)),
            pl.BlockSpec((None, n_heads, seq, v_dim), lambda b, i: (b, 0, 0, 0)),
            pl.BlockSpec((None, tq, n_heads * v_dim), lambda b, i: (b, i, 0)),
            pl.BlockSpec((None, tq, d_model), lambda b, i: (b, i, 0)),
            pl.BlockSpec(wout.shape, lambda b, i: (0, 0)),
            pl.BlockSpec(fn.shape, lambda b, i: (0, 0)),
        ],
        out_specs=pl.BlockSpec((None, tq, d_model), lambda b, i: (b, i, 0)),
        scratch_shapes=[
            pltpu.VMEM((n_heads, tq, LANES), F32),
            pltpu.VMEM((n_heads, tq, v_dim + LANES), F32),
        ],
        compiler_params=pltpu.CompilerParams(
            dimension_semantics=("parallel", "arbitrary"), vmem_limit_bytes=VMEM_LIMIT_BYTES),
        name="mla_attn",
    )(q, k, v, sg, x1, wout, fn)


def _rope_table(seq, rope):
    pos = jnp.arange(seq, dtype=F32)
    inv = ROPE_THETA ** (-jnp.arange(0, rope, 2, dtype=F32) / rope)
    ang = pos[:, None] * inv[None, :]
    cos, sin = jnp.cos(ang), jnp.sin(ang)
    return jnp.concatenate([cos, cos, -sin, sin], axis=1)


def kernel(x, norm_a, w_in_a, conv_w, conv_b, w_rg, b_rg, w_ig, b_ig, lru_lambda, w_out_a,
           norm_kv, w_dkv, kv_norm, w_uk, w_uv, norm_b, w_in_b, q_norm, w_uq, w_out_b,
           final_norm):
    bsz, seq, d_model = x.shape
    n_a = w_in_a.shape[0]
    n_b = w_in_b.shape[0]
    assert n_b == 1, "the final norm is fused into the (single) MLA layer"
    nope = w_uk.shape[-1]
    rope = w_dkv.shape[1] - kv_norm.shape[0]
    scale = float(nope + rope) ** -0.5

    x1 = x
    for layer in range(n_a):
        x1 = _rglru_layer(x1, norm_a[layer].reshape(1, -1), w_in_a[layer], conv_w[layer],
                          conv_b[layer], w_rg[layer], b_rg[layer], w_ig[layer], b_ig[layer],
                          lru_lambda[layer], w_out_a[layer])

    cs = _rope_table(seq, rope)
    q, k, v, sg = _mla_proj(x1, cs, norm_kv, w_dkv, kv_norm, w_uk, w_uv, norm_b[0],
                            w_in_b[0], q_norm[0], w_uq[0])
    return _mla_attn(q, k, v, sg, x1, w_out_b[0], final_norm, scale=scale)
```

```python
import functools

import jax
import jax.numpy as jnp
from jax import lax
from jax.experimental import pallas as pl
from jax.experimental.pallas import tpu as pltpu

F32 = jnp.float32
BF16 = jnp.bfloat16

EPS = 1e-6
LRU_C = 8.0
ROPE_THETA = 10000.0
CONV_WIDTH = 4
NEG_BIG = -1e30
LOG2_E = 1.4426950408889634
LANES = 128

VMEM_LIMIT_BYTES = 56 * 1024 * 1024

RNN_COLS = 256
GATE_BW = 128
TIME_CHUNK = 32
PROJ_ROWS = 512
ATTN_TQ = 512


def _rms(x, g):
    ms = jnp.mean(x * x, axis=-1, keepdims=True)
    return x * lax.rsqrt(ms + EPS) * g


def _sigmoid(x):
    return 1.0 / (1.0 + jnp.exp2(x * (-LOG2_E)))


def _sqrt_unit(t):
    return jnp.where(t > 0.0, t * lax.rsqrt(t), 0.0)


def _rglru_kernel(x_hbm, norm_ref, winx_ref, wing_ref, convw_ref, convb_ref, wg_ref,
                  brg_ref, big_ref, lam_ref, wout_ref, o_hbm,
                  xin, xout, in_sem, out_sem, h_scr, xbuf, y_scr, hist, hstate,
                  *, batch, tchunk, ncol, nsteps):
    rows = batch * tchunk
    halo = (CONV_WIDTH - 1) * batch
    step = pl.program_id(0)
    slot = lax.rem(step, 2)

    def in_copies(s, sl):
        return [pltpu.make_async_copy(x_hbm.at[b, pl.ds(s * tchunk, tchunk), :],
                                      xin.at[sl, :, b, :], in_sem.at[sl]) for b in range(batch)]

    def out_copies(s, sl):
        return [pltpu.make_async_copy(xout.at[sl, :, b, :],
                                      o_hbm.at[b, pl.ds(s * tchunk, tchunk), :], out_sem.at[sl])
                for b in range(batch)]

    @pl.when(step == 0)
    def _():
        hist[...] = jnp.zeros_like(hist)
        hstate[...] = jnp.zeros_like(hstate)
        for c in in_copies(0, 0):
            c.start()

    @pl.when(step + 1 < nsteps)
    def _():
        for c in in_copies(step + 1, 1 - slot):
            c.start()

    for c in in_copies(step, slot):
        c.wait()

    x = xin[slot].reshape(rows, xin.shape[-1])
    h_scr[...] = _rms(x, norm_ref[...]).astype(BF16)

    def col_block(j):
        hb = h_scr[...]
        xb = jnp.dot(hb, winx_ref[j], preferred_element_type=F32)
        xbuf[0:halo, :] = hist[j]
        xbuf[halo:halo + rows, :] = xb
        hist[j] = xb[rows - halo:, :]
        cw = convw_ref[j]
        xc = convb_ref[j] + cw[3:4, :] * xb
        for k in range(CONV_WIDTH - 1):
            xc = xc + cw[k:k + 1, :] * xbuf[k * batch:k * batch + rows, :]
        xcb = xc.astype(BF16)
        g0 = jnp.dot(xcb[:, :GATE_BW], wg_ref[2 * j], preferred_element_type=F32)
        g1 = jnp.dot(xcb[:, GATE_BW:], wg_ref[2 * j + 1], preferred_element_type=F32)
        r_pre = jnp.concatenate([g0[:, :GATE_BW], g1[:, :GATE_BW]], axis=1) + brg_ref[j]
        i_pre = jnp.concatenate([g0[:, GATE_BW:], g1[:, GATE_BW:]], axis=1) + big_ref[j]
        r = _sigmoid(r_pre)
        ig = _sigmoid(i_pre)
        z = -lam_ref[j]
        softplus = jnp.maximum(z, 0.0) + jnp.log1p(jnp.exp(-jnp.abs(z)))
        log_a = r * (-LRU_C * softplus)
        a = jnp.exp(log_a)
        bt = _sqrt_unit(1.0 - a * a) * (ig * xc)
        hst = hstate[j]
        hs = []
        for t in range(tchunk):
            sl = slice(t * batch, (t + 1) * batch)
            hst = a[sl, :] * hst + bt[sl, :]
            hs.append(hst)
        hstate[j] = hst
        hs = jnp.concatenate(hs, axis=0)
        gate = jnp.dot(hb, wing_ref[j], preferred_element_type=F32)
        y_scr[j] = (hs * (gate * _sigmoid(gate))).astype(BF16)

    for j in range(ncol):
        col_block(j)

    acc = x
    for j in range(ncol):
        acc = acc + jnp.dot(y_scr[j], wout_ref[j], preferred_element_type=F32)

    @pl.when(step >= 2)
    def _():
        for c in out_copies(step - 2, slot):
            c.wait()

    xout[slot] = acc.reshape(tchunk, batch, xout.shape[-1])
    for c in out_copies(step, slot):
        c.start()

    @pl.when(step == nsteps - 1)
    def _():
        for c in out_copies(step - 1, 1 - slot):
            c.wait()
        for c in out_copies(step, slot):
            c.wait()


def _rglru_layer(x, norm_g, w_in, conv_w, conv_b, w_rg, b_rg, w_ig, b_ig, lam, w_out):
    batch, seq, d_model = x.shape
    d_rnn = w_out.shape[0]
    assert d_rnn % RNN_COLS == 0 and w_rg.shape[1] == GATE_BW
    ncol = d_rnn // RNN_COLS
    rows = batch * TIME_CHUNK
    nsteps = seq // TIME_CHUNK
    assert seq % TIME_CHUNK == 0 and nsteps >= 2 and batch % 8 == 0
    halo = (CONV_WIDTH - 1) * batch

    def cols(w):
        lead = w.shape[:-1]
        w = w.reshape(lead + (ncol, RNN_COLS))
        return jnp.moveaxis(w, -2, 0)

    winx = cols(w_in[:, :d_rnn]).astype(BF16)
    wing = cols(w_in[:, d_rnn:]).astype(BF16)
    convw = cols(conv_w)
    convb = cols(conv_b[None, :])
    wg = jnp.concatenate([w_rg, w_ig], axis=-1).astype(BF16)
    brg = cols(b_rg[None, :])
    big = cols(b_ig[None, :])
    lamc = cols(lam[None, :])
    wout = w_out.reshape(ncol, RNN_COLS, d_model).astype(BF16)

    full = lambda a: pl.BlockSpec(a.shape, lambda i: (0,) * a.ndim)
    kern = functools.partial(_rglru_kernel, batch=batch, tchunk=TIME_CHUNK, ncol=ncol,
                             nsteps=nsteps)
    return pl.pallas_call(
        kern,
        out_shape=jax.ShapeDtypeStruct((batch, seq, d_model), F32),
        grid=(nsteps,),
        in_specs=[
            pl.BlockSpec(memory_space=pl.ANY),
            full(norm_g), full(winx), full(wing), full(convw), full(convb), full(wg),
            full(brg), full(big), full(lamc), full(wout),
        ],
        out_specs=pl.BlockSpec(memory_space=pl.ANY),
        scratch_shapes=[
            pltpu.VMEM((2, TIME_CHUNK, batch, d_model), F32),
            pltpu.VMEM((2, TIME_CHUNK, batch, d_model), F32),
            pltpu.SemaphoreType.DMA((2,)),
            pltpu.SemaphoreType.DMA((2,)),
            pltpu.VMEM((rows, d_model), BF16),
            pltpu.VMEM((halo + rows, RNN_COLS), F32),
            pltpu.VMEM((ncol, rows, RNN_COLS), BF16),
            pltpu.VMEM((ncol, halo, RNN_COLS), F32),
            pltpu.VMEM((ncol, batch, RNN_COLS), F32),
        ],
        compiler_params=pltpu.CompilerParams(
            dimension_semantics=("arbitrary",), vmem_limit_bytes=VMEM_LIMIT_BYTES),
        name="rglru_layer",
    )(x, norm_g, winx, wing, convw, convb, wg, brg, big, lamc, wout)


def _proj_kernel(x_ref, cs_ref, nkv_ref, nb_ref, wdkv_ref, kvn_ref, wuk_ref, wuv_ref,
                 winq_ref, wing_ref, qn_ref, wuq_ref,
                 q_ref, k_ref, v_ref, sg_ref, *, n_heads, kv_rank, nope, rope, v_dim):
    x = x_ref[...]
    ms = jnp.mean(x * x, axis=-1, keepdims=True)
    xn = x * lax.rsqrt(ms + EPS)
    cs = cs_ref[...]

    def roped(tile):
        p = tile * cs
        return (p + pltpu.roll(p, rope, axis=1))[:, :rope]

    hk = (xn * nkv_ref[...]).astype(BF16)
    ckr = jnp.dot(hk, wdkv_ref[...], preferred_element_type=F32)
    c_kv = _rms(ckr[:, :kv_rank], kvn_ref[...]).astype(BF16)
    k_rope = roped(ckr[:, kv_rank:]).astype(BF16)
    k_nope = jnp.dot(c_kv, wuk_ref[...], preferred_element_type=F32)
    v = jnp.dot(c_kv, wuv_ref[...], preferred_element_type=F32)
    for h in range(n_heads):
        k_ref[h, :, 0:nope] = k_nope[:, h * nope:(h + 1) * nope].astype(BF16)
        k_ref[h, :, nope:nope + rope] = k_rope
        v_ref[h] = v[:, h * v_dim:(h + 1) * v_dim].astype(BF16)

    hq = (xn * nb_ref[...]).astype(BF16)
    cq = jnp.dot(hq, winq_ref[...], preferred_element_type=F32)
    c_q = _rms(cq, qn_ref[...]).astype(BF16)
    gate = jnp.dot(hq, wing_ref[...], preferred_element_type=F32)
    sg_ref[...] = (gate * _sigmoid(gate)).astype(BF16)
    qw = nope + 2 * rope
    for h in range(n_heads):
        qh = jnp.dot(c_q, wuq_ref[:, h * qw:(h + 1) * qw], preferred_element_type=F32)
        q_ref[h, :, 0:nope] = qh[:, :nope].astype(BF16)
        q_ref[h, :, nope:nope + rope] = roped(qh[:, nope:]).astype(BF16)


def _swap_halves(w):
    half = w.shape[-1] // 2
    return jnp.concatenate([w[..., half:], w[..., :half]], axis=-1)


def _mla_proj(x1, cs, norm_kv, w_dkv, kv_norm, w_uk, w_uv, norm_b, w_in, q_norm, w_uq):
    bsz, seq, d_model = x1.shape
    kv_rank = kv_norm.shape[0]
    rope = w_dkv.shape[1] - kv_rank
    _, n_heads, nope = w_uk.shape
    v_dim = w_uv.shape[2]
    q_rank = q_norm.shape[0]
    assert 2 * rope == 128 and nope % 128 == 0 and seq % PROJ_ROWS == 0

    wr = w_dkv[:, kv_rank:]
    wdkv = jnp.concatenate([w_dkv[:, :kv_rank], wr, _swap_halves(wr)], axis=1).astype(BF16)
    wuk = w_uk.reshape(kv_rank, n_heads * nope).astype(BF16)
    wuv = w_uv.reshape(kv_rank, n_heads * v_dim).astype(BF16)
    winq = w_in[:, :q_rank].astype(BF16)
    wing = w_in[:, q_rank:].astype(BF16)
    wqr = w_uq[:, :, nope:]
    wuq = jnp.concatenate([w_uq[:, :, :nope], wqr, _swap_halves(wqr)], axis=-1)
    wuq = wuq.reshape(q_rank, n_heads * (nope + 2 * rope)).astype(BF16)

    r2 = lambda a: a.reshape(1, -1)
    args = (x1, cs, r2(norm_kv), r2(norm_b), wdkv, r2(kv_norm), wuk, wuv, winq, wing,
            r2(q_norm), wuq)
    full = lambda a: pl.BlockSpec(a.shape, lambda b, i: (0,) * a.ndim)
    qk = nope + rope
    kern = functools.partial(_proj_kernel, n_heads=n_heads, kv_rank=kv_rank, nope=nope,
                             rope=rope, v_dim=v_dim)
    head_out = lambda d: pl.BlockSpec((None, n_heads, PROJ_ROWS, d), lambda b, i: (b, 0, i, 0))
    return pl.pallas_call(
        kern,
        out_shape=(
            jax.ShapeDtypeStruct((bsz, n_heads, seq, qk), BF16),
            jax.ShapeDtypeStruct((bsz, n_heads, seq, qk), BF16),
            jax.ShapeDtypeStruct((bsz, n_heads, seq, v_dim), BF16),
            jax.ShapeDtypeStruct((bsz, seq, n_heads * v_dim), BF16),
        ),
        grid=(bsz, seq // PROJ_ROWS),
        in_specs=[
            pl.BlockSpec((None, PROJ_ROWS, d_model), lambda b, i: (b, i, 0)),
            pl.BlockSpec((PROJ_ROWS, 4 * (rope // 2)), lambda b, i: (i, 0)),
        ] + [full(a) for a in args[2:]],
        out_specs=(
            head_out(qk), head_out(qk), head_out(v_dim),
            pl.BlockSpec((None, PROJ_ROWS, n_heads * v_dim), lambda b, i: (b, i, 0)),
        ),
        compiler_params=pltpu.CompilerParams(
            dimension_semantics=("parallel", "parallel"), vmem_limit_bytes=VMEM_LIMIT_BYTES),
        name="mla_proj",
    )(*args)


def _attn_kernel(q_ref, k_ref, v_ref, sg_ref, x_ref, wout_ref, fn_ref, o_ref,
                 m_scr, acc_scr, *, n_heads, tq, exp2_scale):
    qi = pl.program_id(1)
    half = tq // 2

    def attend(r0, nrows, kstart, width, diag_off, first):
        if diag_off is not None:
            row = lax.broadcasted_iota(jnp.int32, (nrows, width), 0)
            col = lax.broadcasted_iota(jnp.int32, (nrows, width), 1)
            keep = col <= row + diag_off
        ones = jnp.ones((width, LANES), BF16)
        for h in range(n_heads):
            q = q_ref[h, r0:r0 + nrows, :]
            k = k_ref[h, pl.ds(kstart, width), :]
            v = jnp.concatenate([v_ref[h, pl.ds(kstart, width), :], ones], axis=1)
            s = lax.dot_general(q, k, (((1,), (1,)), ((), ())),
                                preferred_element_type=F32)
            if diag_off is not None:
                s = jnp.where(keep, s, NEG_BIG)
            m_cur = jnp.max(s, axis=-1, keepdims=True)
            if first:
                m_new = jnp.broadcast_to(m_cur, (nrows, LANES))
            else:
                m_prev = m_scr[h, r0:r0 + nrows, :]
                m_new = jnp.maximum(m_prev, m_cur)
            p = jnp.exp2((s - jnp.concatenate([m_new] * (width // LANES), axis=1)) * exp2_scale)
            pv = jnp.dot(p.astype(BF16), v, preferred_element_type=F32)
            if not first:
                alpha = jnp.exp2((m_prev - m_new) * exp2_scale)
                pv = jnp.concatenate([alpha, alpha], axis=1) * acc_scr[h, r0:r0 + nrows, :] + pv
            acc_scr[h, r0:r0 + nrows, :] = pv
            m_scr[h, r0:r0 + nrows, :] = m_new

    diag = pl.multiple_of(qi * tq, tq)
    attend(0, half, diag, half, 0, first=True)
    attend(half, half, diag, tq, half, first=True)

    def full_tile(j, carry):
        attend(0, tq, pl.multiple_of(j * tq, tq), tq, None, first=False)
        return carry

    lax.fori_loop(0, qi, full_tile, 0)

    o = jnp.concatenate([acc_scr[h, :, :LANES] / acc_scr[h, :, LANES:] for h in range(n_heads)],
                        axis=1)
    y = (o * sg_ref[...].astype(F32)).astype(BF16)
    out = x_ref[...] + jnp.dot(y, wout_ref[...], preferred_element_type=F32)
    o_ref[...] = _rms(out, fn_ref[...])


def _mla_attn(q, k, v, sg, x1, w_out, final_norm, *, scale):
    bsz, n_heads, seq, qk = q.shape
    v_dim = v.shape[-1]
    d_model = x1.shape[-1]
    tq = ATTN_TQ
    assert seq % tq == 0
    assert v_dim == LANES and tq % LANES == 0
    kern = functools.partial(_attn_kernel, n_heads=n_heads, tq=tq,
                             exp2_scale=scale * LOG2_E)
    wout = w_out.astype(BF16)
    fn = final_norm.reshape(1, -1)
    return pl.pallas_call(
        kern,
        out_shape=jax.ShapeDtypeStruct((bsz, seq, d_model), F32),
        grid=(bsz, seq // tq),
        in_specs=[
            pl.BlockSpec((None, n_heads, tq, qk), lambda b, i: (b, 0, i, 0)),
            pl.BlockSpec((None, n_heads, seq, qk), lambda b, i: (b, 0, 0, 0)),
            pl.BlockSpec((None, n_heads, seq, v_dim), lambda b, i: (b, 0, 0, 0)),
            pl.BlockSpec((None, tq, n_heads * v_dim), lambda b, i: (b, i, 0)),
            pl.BlockSpec((None, tq, d_model), lambda b, i: (b, i, 0)),
            pl.BlockSpec(wout.shape, lambda b, i: (0, 0)),
            pl.BlockSpec(fn.shape, lambda b, i: (0, 0)),
        ],
        out_specs=pl.BlockSpec((None, tq, d_model), lambda b, i: (b, i, 0)),
        scratch_shapes=[
            pltpu.VMEM((n_heads, tq, LANES), F32),
            pltpu.VMEM((n_heads, tq, v_dim + LANES), F32),
        ],
        compiler_params=pltpu.CompilerParams(
            dimension_semantics=("parallel", "arbitrary"), vmem_limit_bytes=VMEM_LIMIT_BYTES),
        name="mla_attn",
    )(q, k, v, sg, x1, wout, fn)


def _rope_table(seq, rope):
    pos = jnp.arange(seq, dtype=F32)
    inv = ROPE_THETA ** (-jnp.arange(0, rope, 2, dtype=F32) / rope)
    ang = pos[:, None] * inv[None, :]
    cos, sin = jnp.cos(ang), jnp.sin(ang)
    return jnp.concatenate([cos, cos, -sin, sin], axis=1)


def kernel(x, norm_a, w_in_a, conv_w, conv_b, w_rg, b_rg, w_ig, b_ig, lru_lambda, w_out_a,
           norm_kv, w_dkv, kv_norm, w_uk, w_uv, norm_b, w_in_b, q_norm, w_uq, w_out_b,
           final_norm):
    bsz, seq, d_model = x.shape
    n_a = w_in_a.shape[0]
    n_b = w_in_b.shape[0]
    assert n_b == 1, "the final norm is fused into the (single) MLA layer"
    nope = w_uk.shape[-1]
    rope = w_dkv.shape[1] - kv_norm.shape[0]
    scale = float(nope + rope) ** -0.5

    x1 = x
    for layer in range(n_a):
        x1 = _rglru_layer(x1, norm_a[layer].reshape(1, -1), w_in_a[layer], conv_w[layer],
                          conv_b[layer], w_rg[layer], b_rg[layer], w_ig[layer], b_ig[layer],
                          lru_lambda[layer], w_out_a[layer])

    cs = _rope_table(seq, rope)
    q, k, v, sg = _mla_proj(x1, cs, norm_kv, w_dkv, kv_norm, w_uk, w_uv, norm_b[0],
                            w_in_b[0], q_norm[0], w_uq[0])
    return _mla_attn(q, k, v, sg, x1, w_out_b[0], final_norm, scale=scale)
```

```python
import functools

import jax
import jax.numpy as jnp
from jax import lax
from jax.experimental import pallas as pl
from jax.experimental.pallas import tpu as pltpu

F32 = jnp.float32
BF16 = jnp.bfloat16

EPS = 1e-6
LRU_C = 8.0
ROPE_THETA = 10000.0
CONV_WIDTH = 4
NEG_BIG = -1e30
LOG2_E = 1.4426950408889634
LANES = 128

VMEM_LIMIT_BYTES = 56 * 1024 * 1024

RNN_COLS = 256
GATE_BW = 128
TIME_CHUNK = 32
PROJ_ROWS = 1024
ATTN_TQ = 512


def _rms(x, g):
    ms = jnp.mean(x * x, axis=-1, keepdims=True)
    return x * lax.rsqrt(ms + EPS) * g


def _sigmoid(x):
    return 1.0 / (1.0 + jnp.exp2(x * (-LOG2_E)))


def _sqrt_unit(t):
    return jnp.where(t > 0.0, t * lax.rsqrt(t), 0.0)


def _rglru_kernel(x_hbm, norm_ref, win_ref, convw_ref, convb_ref, wg_ref,
                  brg_ref, big_ref, lam_ref, wout_ref, o_hbm,
                  xin, xout, in_sem, out_sem, h_scr, xbuf, y_scr, hist, hstate,
                  *, batch, tchunk, ncol, nsteps):
    rows = batch * tchunk
    halo = (CONV_WIDTH - 1) * batch
    d_rnn = ncol * RNN_COLS
    step = pl.program_id(0)
    slot = lax.rem(step, 2)

    def in_copies(s, sl):
        return [pltpu.make_async_copy(x_hbm.at[b, pl.ds(s * tchunk, tchunk), :],
                                      xin.at[sl, :, b, :], in_sem.at[sl]) for b in range(batch)]

    def out_copies(s, sl):
        return [pltpu.make_async_copy(xout.at[sl, :, b, :],
                                      o_hbm.at[b, pl.ds(s * tchunk, tchunk), :], out_sem.at[sl])
                for b in range(batch)]

    @pl.when(step == 0)
    def _():
        hist[...] = jnp.zeros_like(hist)
        hstate[...] = jnp.zeros_like(hstate)
        for c in in_copies(0, 0):
            c.start()

    @pl.when(step + 1 < nsteps)
    def _():
        for c in in_copies(step + 1, 1 - slot):
            c.start()

    @pl.when(step >= 2)
    def _():
        for c in out_copies(step - 2, slot):
            c.wait()

    for c in in_copies(step, slot):
        c.wait()

    d_model = xin.shape[-1]
    x = xin[slot].reshape(rows, d_model)
    h_scr[...] = _rms(x, norm_ref[...]).astype(BF16)

    def col_block(j):
        hb = h_scr[...]
        c0 = j * RNN_COLS
        xb = jnp.dot(hb, win_ref[:, c0:c0 + RNN_COLS], preferred_element_type=F32)
        xbuf[0:halo, :] = hist[j]
        xbuf[halo:halo + rows, :] = xb
        hist[j] = xb[rows - halo:, :]
        cw = convw_ref[j]
        xc = convb_ref[j] + cw[3:4, :] * xb
        for k in range(CONV_WIDTH - 1):
            xc = xc + cw[k:k + 1, :] * xbuf[k * batch:k * batch + rows, :]
        xcb = xc.astype(BF16)
        g0 = jnp.dot(xcb[:, :GATE_BW], wg_ref[2 * j], preferred_element_type=F32)
        g1 = jnp.dot(xcb[:, GATE_BW:], wg_ref[2 * j + 1], preferred_element_type=F32)
        r_pre = jnp.concatenate([g0[:, :GATE_BW], g1[:, :GATE_BW]], axis=1) + brg_ref[j]
        i_pre = jnp.concatenate([g0[:, GATE_BW:], g1[:, GATE_BW:]], axis=1) + big_ref[j]
        r = _sigmoid(r_pre)
        ig = _sigmoid(i_pre)
        z = -lam_ref[j]
        softplus = jnp.maximum(z, 0.0) + jnp.log1p(jnp.exp(-jnp.abs(z)))
        log_a = r * (-LRU_C * softplus)
        a = jnp.exp(log_a)
        bt = _sqrt_unit(1.0 - a * a) * (ig * xc)
        hst = hstate[j]
        hs = []
        for t in range(tchunk):
            sl = slice(t * batch, (t + 1) * batch)
            hst = a[sl, :] * hst + bt[sl, :]
            hs.append(hst)
        hstate[j] = hst
        hs = jnp.concatenate(hs, axis=0)
        gate = jnp.dot(hb, win_ref[:, d_rnn + c0:d_rnn + c0 + RNN_COLS],
                       preferred_element_type=F32)
        y_scr[j] = (hs * (gate * _sigmoid(gate))).astype(BF16)

    for j in range(ncol):
        col_block(j)

    half = d_model // 2
    for c0 in (0, half):
        acc = x[:, c0:c0 + half]
        for j in range(ncol):
            acc = acc + jnp.dot(y_scr[j], wout_ref[j, :, c0:c0 + half],
                                preferred_element_type=F32)
        xout[slot, :, :, c0:c0 + half] = acc.reshape(tchunk, batch, half)
    for c in out_copies(step, slot):
        c.start()

    @pl.when(step == nsteps - 1)
    def _():
        for c in out_copies(step - 1, 1 - slot):
            c.wait()
        for c in out_copies(step, slot):
            c.wait()


def _rglru_layer(x, norm_g, w_in, conv_w, conv_b, w_rg, b_rg, w_ig, b_ig, lam, w_out):
    batch, seq, d_model = x.shape
    d_rnn = w_out.shape[0]
    assert d_rnn % RNN_COLS == 0 and w_rg.shape[1] == GATE_BW
    ncol = d_rnn // RNN_COLS
    rows = batch * TIME_CHUNK
    nsteps = seq // TIME_CHUNK
    assert seq % TIME_CHUNK == 0 and nsteps >= 2 and batch % 8 == 0
    halo = (CONV_WIDTH - 1) * batch

    def cols(w):
        lead = w.shape[:-1]
        w = w.reshape(lead + (ncol, RNN_COLS))
        return jnp.moveaxis(w, -2, 0)

    win = w_in.astype(BF16)
    convw = cols(conv_w)
    convb = cols(conv_b[None, :])
    wg = jnp.concatenate([w_rg, w_ig], axis=-1).astype(BF16)
    brg = cols(b_rg[None, :])
    big = cols(b_ig[None, :])
    lamc = cols(lam[None, :])
    wout = w_out.reshape(ncol, RNN_COLS, d_model).astype(BF16)

    full = lambda a: pl.BlockSpec(a.shape, lambda i: (0,) * a.ndim)
    kern = functools.partial(_rglru_kernel, batch=batch, tchunk=TIME_CHUNK, ncol=ncol,
                             nsteps=nsteps)
    return pl.pallas_call(
        kern,
        out_shape=jax.ShapeDtypeStruct((batch, seq, d_model), F32),
        grid=(nsteps,),
        in_specs=[
            pl.BlockSpec(memory_space=pl.ANY),
            full(norm_g), full(win), full(convw), full(convb), full(wg),
            full(brg), full(big), full(lamc), full(wout),
        ],
        out_specs=pl.BlockSpec(memory_space=pl.ANY),
        scratch_shapes=[
            pltpu.VMEM((2, TIME_CHUNK, batch, d_model), F32),
            pltpu.VMEM((2, TIME_CHUNK, batch, d_model), F32),
            pltpu.SemaphoreType.DMA((2,)),
            pltpu.SemaphoreType.DMA((2,)),
            pltpu.VMEM((rows, d_model), BF16),
            pltpu.VMEM((halo + rows, RNN_COLS), F32),
            pltpu.VMEM((ncol, rows, RNN_COLS), BF16),
            pltpu.VMEM((ncol, halo, RNN_COLS), F32),
            pltpu.VMEM((ncol, batch, RNN_COLS), F32),
        ],
        compiler_params=pltpu.CompilerParams(
            dimension_semantics=("arbitrary",), vmem_limit_bytes=VMEM_LIMIT_BYTES),
        name="rglru_layer",
    )(x, norm_g, win, convw, convb, wg, brg, big, lamc, wout)


def _proj_kernel(x_ref, cs_ref, nkv_ref, nb_ref, wdkv_ref, kvn_ref, wuk_ref, wuv_ref,
                 winq_ref, wing_ref, qn_ref, wuq_ref,
                 q_ref, k_ref, v_ref, sg_ref, *, n_heads, kv_rank, nope, rope, v_dim, q_scale):
    x = x_ref[...]
    ms = jnp.mean(x * x, axis=-1, keepdims=True)
    xn = x * lax.rsqrt(ms + EPS)
    cs = cs_ref[...]

    def roped(tile):
        p = tile * cs
        return (p + pltpu.roll(p, rope, axis=1))[:, :rope]

    hk = (xn * nkv_ref[...]).astype(BF16)
    ckr = jnp.dot(hk, wdkv_ref[...], preferred_element_type=F32)
    c_kv = _rms(ckr[:, :kv_rank], kvn_ref[...]).astype(BF16)
    k_rope = roped(ckr[:, kv_rank:]).astype(BF16)
    k_nope = jnp.dot(c_kv, wuk_ref[...], preferred_element_type=F32)
    v = jnp.dot(c_kv, wuv_ref[...], preferred_element_type=F32)
    for h in range(n_heads):
        k_ref[h, :, 0:nope] = k_nope[:, h * nope:(h + 1) * nope].astype(BF16)
        k_ref[h, :, nope:nope + rope] = k_rope
        v_ref[h] = v[:, h * v_dim:(h + 1) * v_dim].astype(BF16)

    hq = (xn * nb_ref[...]).astype(BF16)
    cq = jnp.dot(hq, winq_ref[...], preferred_element_type=F32)
    c_q = _rms(cq, qn_ref[...] * q_scale).astype(BF16)
    gate = jnp.dot(hq, wing_ref[...], preferred_element_type=F32)
    sg_ref[...] = (gate * _sigmoid(gate)).astype(BF16)
    qw = nope + 2 * rope
    for h in range(n_heads):
        qh = jnp.dot(c_q, wuq_ref[:, h * qw:(h + 1) * qw], preferred_element_type=F32)
        q_ref[h, :, 0:nope] = qh[:, :nope].astype(BF16)
        q_ref[h, :, nope:nope + rope] = roped(qh[:, nope:]).astype(BF16)


def _swap_halves(w):
    half = w.shape[-1] // 2
    return jnp.concatenate([w[..., half:], w[..., :half]], axis=-1)


def _mla_proj(x1, cs, norm_kv, w_dkv, kv_norm, w_uk, w_uv, norm_b, w_in, q_norm, w_uq,
              *, q_scale):
    bsz, seq, d_model = x1.shape
    kv_rank = kv_norm.shape[0]
    rope = w_dkv.shape[1] - kv_rank
    _, n_heads, nope = w_uk.shape
    v_dim = w_uv.shape[2]
    q_rank = q_norm.shape[0]
    assert 2 * rope == 128 and nope % 128 == 0 and seq % PROJ_ROWS == 0

    wr = w_dkv[:, kv_rank:]
    wdkv = jnp.concatenate([w_dkv[:, :kv_rank], wr, _swap_halves(wr)], axis=1).astype(BF16)
    wuk = w_uk.reshape(kv_rank, n_heads * nope).astype(BF16)
    wuv = w_uv.reshape(kv_rank, n_heads * v_dim).astype(BF16)
    winq = w_in[:, :q_rank].astype(BF16)
    wing = w_in[:, q_rank:].astype(BF16)
    wqr = w_uq[:, :, nope:]
    wuq = jnp.concatenate([w_uq[:, :, :nope], wqr, _swap_halves(wqr)], axis=-1)
    wuq = wuq.reshape(q_rank, n_heads * (nope + 2 * rope)).astype(BF16)

    r2 = lambda a: a.reshape(1, -1)
    args = (x1, cs, r2(norm_kv), r2(norm_b), wdkv, r2(kv_norm), wuk, wuv, winq, wing,
            r2(q_norm), wuq)
    full = lambda a: pl.BlockSpec(a.shape, lambda b, i: (0,) * a.ndim)
    qk = nope + rope
    kern = functools.partial(_proj_kernel, n_heads=n_heads, kv_rank=kv_rank, nope=nope,
                             rope=rope, v_dim=v_dim, q_scale=q_scale)
    head_out = lambda d: pl.BlockSpec((None, n_heads, PROJ_ROWS, d), lambda b, i: (b, 0, i, 0))
    return pl.pallas_call(
        kern,
        out_shape=(
            jax.ShapeDtypeStruct((bsz, n_heads, seq, qk), BF16),
            jax.ShapeDtypeStruct((bsz, n_heads, seq, qk), BF16),
            jax.ShapeDtypeStruct((bsz, n_heads, seq, v_dim), BF16),
            jax.ShapeDtypeStruct((bsz, seq, n_heads * v_dim), BF16),
        ),
        grid=(bsz, seq // PROJ_ROWS),
        in_specs=[
            pl.BlockSpec((None, PROJ_ROWS, d_model), lambda b, i: (b, i, 0)),
            pl.BlockSpec((PROJ_ROWS, 4 * (rope // 2)), lambda b, i: (i, 0)),
        ] + [full(a) for a in args[2:]],
        out_specs=(
            head_out(qk), head_out(qk), head_out(v_dim),
            pl.BlockSpec((None, PROJ_ROWS, n_heads * v_dim), lambda b, i: (b, i, 0)),
        ),
        compiler_params=pltpu.CompilerParams(
            dimension_semantics=("parallel", "parallel"), vmem_limit_bytes=VMEM_LIMIT_BYTES),
        name="mla_proj",
    )(*args)


def _attn_kernel(q_ref, k_ref, v_ref, sg_ref, x_ref, wout_ref, fn_ref, o_ref,
                 m_scr, acc_scr, *, n_heads, tq):
    qi = pl.program_id(1)
    half = tq // 2

    def attend(r0, nrows, kstart, width, diag_off, first):
        if diag_off is not None:
            row = lax.broadcasted_iota(jnp.int32, (nrows, width), 0)
            col = lax.broadcasted_iota(jnp.int32, (nrows, width), 1)
            keep = col <= row + diag_off
        ones = jnp.ones((width, LANES), BF16)
        for h in range(n_heads):
            q = q_ref[h, r0:r0 + nrows, :]
            k = k_ref[h, pl.ds(kstart, width), :]
            v = jnp.concatenate([v_ref[h, pl.ds(kstart, width), :], ones], axis=1)
            s = lax.dot_general(q, k, (((1,), (1,)), ((), ())),
                                preferred_element_type=F32)
            if diag_off is not None:
                s = jnp.where(keep, s, NEG_BIG)
            m_cur = jnp.max(s, axis=-1, keepdims=True)
            if first:
                m_new = jnp.broadcast_to(m_cur, (nrows, LANES))
            else:
                m_prev = m_scr[h, r0:r0 + nrows, :]
                m_new = jnp.maximum(m_prev, m_cur)
            p = jnp.exp2(s - jnp.concatenate([m_new] * (width // LANES), axis=1))
            pv = jnp.dot(p.astype(BF16), v, preferred_element_type=F32)
            if not first:
                alpha = jnp.exp2(m_prev - m_new)
                pv = jnp.concatenate([alpha, alpha], axis=1) * acc_scr[h, r0:r0 + nrows, :] + pv
            acc_scr[h, r0:r0 + nrows, :] = pv
            m_scr[h, r0:r0 + nrows, :] = m_new

    diag = pl.multiple_of(qi * tq, tq)
    attend(0, half, diag, half, 0, first=True)
    attend(half, half, diag, tq, half, first=True)

    def full_pair(j, carry):
        attend(0, tq, pl.multiple_of(2 * j * tq, tq), tq, None, first=False)
        attend(0, tq, pl.multiple_of((2 * j + 1) * tq, tq), tq, None, first=False)
        return carry

    lax.fori_loop(0, qi // 2, full_pair, 0)

    @pl.when(qi % 2 == 1)
    def _():
        attend(0, tq, pl.multiple_of((qi - 1) * tq, tq), tq, None, first=False)

    o = jnp.concatenate([acc_scr[h, :, :LANES] / acc_scr[h, :, LANES:] for h in range(n_heads)],
                        axis=1)
    y = (o * sg_ref[...].astype(F32)).astype(BF16)
    out = x_ref[...] + jnp.dot(y, wout_ref[...], preferred_element_type=F32)
    o_ref[...] = _rms(out, fn_ref[...])


def _mla_attn(q, k, v, sg, x1, w_out, final_norm):
    bsz, n_heads, seq, qk = q.shape
    v_dim = v.shape[-1]
    d_model = x1.shape[-1]
    tq = ATTN_TQ
    assert seq % tq == 0
    assert v_dim == LANES and tq % LANES == 0
    kern = functools.partial(_attn_kernel, n_heads=n_heads, tq=tq)
    wout = w_out.astype(BF16)
    fn = final_norm.reshape(1, -1)
    return pl.pallas_call(
        kern,
        out_shape=jax.ShapeDtypeStruct((bsz, seq, d_model), F32),
        grid=(bsz, seq // tq),
        in_specs=[
            pl.BlockSpec((None, n_heads, tq, qk), lambda b, i: (b, 0, i, 0)),
            pl.BlockSpec((None, n_heads, seq, qk), lambda b, i: (b, 0, 0, 0)),
            pl.BlockSpec((None, n_heads, seq, v_dim), lambda b, i: (b, 0, 0, 0)),
            pl.BlockSpec((None, tq, n_heads * v_dim), lambda b, i: (b, i, 0)),
            pl.BlockSpec((None, tq, d_model), lambda b, i: (b, i, 0)),
            pl.BlockSpec(wout.shape, lambda b, i: (0, 0)),
            pl.BlockSpec(fn.shape, lambda b, i: (0, 0)),
        ],
        out_specs=pl.BlockSpec((None, tq, d_model), lambda b, i: (b, i, 0)),
        scratch_shapes=[
            pltpu.VMEM((n_heads, tq, LANES), F32),
            pltpu.VMEM((n_heads, tq, v_dim + LANES), F32),
        ],
        compiler_params=pltpu.CompilerParams(
            dimension_semantics=("parallel", "arbitrary"), vmem_limit_bytes=VMEM_LIMIT_BYTES),
        name="mla_attn",
    )(q, k, v, sg, x1, wout, fn)


def _rope_table(seq, rope):
    pos = jnp.arange(seq, dtype=F32)
    inv = ROPE_THETA ** (-jnp.arange(0, rope, 2, dtype=F32) / rope)
    ang = pos[:, None] * inv[None, :]
    cos, sin = jnp.cos(ang), jnp.sin(ang)
    return jnp.concatenate([cos, cos, -sin, sin], axis=1)


def kernel(x, norm_a, w_in_a, conv_w, conv_b, w_rg, b_rg, w_ig, b_ig, lru_lambda, w_out_a,
           norm_kv, w_dkv, kv_norm, w_uk, w_uv, norm_b, w_in_b, q_norm, w_uq, w_out_b,
           final_norm):
    bsz, seq, d_model = x.shape
    n_a = w_in_a.shape[0]
    n_b = w_in_b.shape[0]
    assert n_b == 1, "the final norm is fused into the (single) MLA layer"
    nope = w_uk.shape[-1]
    rope = w_dkv.shape[1] - kv_norm.shape[0]
    scale = float(nope + rope) ** -0.5

    x1 = x
    for layer in range(n_a):
        x1 = _rglru_layer(x1, norm_a[layer].reshape(1, -1), w_in_a[layer], conv_w[layer],
                          conv_b[layer], w_rg[layer], b_rg[layer], w_ig[layer], b_ig[layer],
                          lru_lambda[layer], w_out_a[layer])

    cs = _rope_table(seq, rope)
    q, k, v, sg = _mla_proj(x1, cs, norm_kv, w_dkv, kv_norm, w_uk, w_uv, norm_b[0],
                            w_in_b[0], q_norm[0], w_uq[0], q_scale=scale * LOG2_E)
    return _mla_attn(q, k, v, sg, x1, w_out_b[0], final_norm)
```

```python
import functools

import jax
import jax.numpy as jnp
from jax import lax
from jax.experimental import pallas as pl
from jax.experimental.pallas import tpu as pltpu

F32 = jnp.float32
BF16 = jnp.bfloat16

EPS = 1e-6
LRU_C = 8.0
ROPE_THETA = 10000.0
CONV_WIDTH = 4
NEG_BIG = -1e30
LOG2_E = 1.4426950408889634
LANES = 128

VMEM_LIMIT_BYTES = 56 * 1024 * 1024

RNN_COLS = 256
GATE_BW = 128
TIME_CHUNK = 32
PROJ_ROWS = 1024
ATTN_TQ = 512


def _rms(x, g):
    ms = jnp.mean(x * x, axis=-1, keepdims=True)
    return x * lax.rsqrt(ms + EPS) * g


def _sigmoid(x):
    return 1.0 / (1.0 + jnp.exp2(x * (-LOG2_E)))


def _sqrt_unit(t):
    return jnp.where(t > 0.0, t * lax.rsqrt(t), 0.0)


def _rglru_kernel(x_hbm, norm_ref, win_ref, convw_ref, convb_ref, wg_ref,
                  brg_ref, big_ref, lam_ref, wout_ref, o_hbm,
                  xin, xout, in_sem, out_sem, h_scr, xbuf, y_scr, hist, hstate,
                  *, batch, tchunk, ncol, nsteps):
    rows = batch * tchunk
    halo = (CONV_WIDTH - 1) * batch
    d_rnn = ncol * RNN_COLS
    step = pl.program_id(0)
    slot = lax.rem(step, 2)

    def in_copies(s, sl):
        return [pltpu.make_async_copy(x_hbm.at[b, pl.ds(s * tchunk, tchunk), :],
                                      xin.at[sl, :, b, :], in_sem.at[sl]) for b in range(batch)]

    def out_copies(s, sl):
        return [pltpu.make_async_copy(xout.at[sl, :, b, :],
                                      o_hbm.at[b, pl.ds(s * tchunk, tchunk), :], out_sem.at[sl])
                for b in range(batch)]

    @pl.when(step == 0)
    def _():
        hist[...] = jnp.zeros_like(hist)
        hstate[...] = jnp.zeros_like(hstate)
        for c in in_copies(0, 0):
            c.start()

    @pl.when(step + 1 < nsteps)
    def _():
        for c in in_copies(step + 1, 1 - slot):
            c.start()

    @pl.when(step >= 2)
    def _():
        for c in out_copies(step - 2, slot):
            c.wait()

    for c in in_copies(step, slot):
        c.wait()

    d_model = xin.shape[-1]
    x = xin[slot].reshape(rows, d_model)
    h_scr[...] = _rms(x, norm_ref[...]).astype(BF16)

    def col_block(j):
        hb = h_scr[...]
        c0 = j * RNN_COLS
        xb = jnp.dot(hb, win_ref[:, c0:c0 + RNN_COLS], preferred_element_type=F32)
        xbuf[0:halo, :] = hist[j]
        xbuf[halo:halo + rows, :] = xb
        hist[j] = xb[rows - halo:, :]
        cw = convw_ref[j]
        xc = convb_ref[j] + cw[3:4, :] * xb
        for k in range(CONV_WIDTH - 1):
            xc = xc + cw[k:k + 1, :] * xbuf[k * batch:k * batch + rows, :]
        xcb = xc.astype(BF16)
        g0 = jnp.dot(xcb[:, :GATE_BW], wg_ref[2 * j], preferred_element_type=F32)
        g1 = jnp.dot(xcb[:, GATE_BW:], wg_ref[2 * j + 1], preferred_element_type=F32)
        r_pre = jnp.concatenate([g0[:, :GATE_BW], g1[:, :GATE_BW]], axis=1) + brg_ref[j]
        i_pre = jnp.concatenate([g0[:, GATE_BW:], g1[:, GATE_BW:]], axis=1) + big_ref[j]
        r = _sigmoid(r_pre)
        ig = _sigmoid(i_pre)
        z = -lam_ref[j]
        softplus = jnp.maximum(z, 0.0) + jnp.log1p(jnp.exp(-jnp.abs(z)))
        log_a = r * (-LRU_C * softplus)
        a = jnp.exp(log_a)
        bt = _sqrt_unit(1.0 - a * a) * (ig * xc)
        hst = hstate[j]
        hs = []
        for t in range(tchunk):
            sl = slice(t * batch, (t + 1) * batch)
            hst = a[sl, :] * hst + bt[sl, :]
            hs.append(hst)
        hstate[j] = hst
        hs = jnp.concatenate(hs, axis=0)
        gate = jnp.dot(hb, win_ref[:, d_rnn + c0:d_rnn + c0 + RNN_COLS],
                       preferred_element_type=F32)
        y_scr[j] = (hs * (gate * _sigmoid(gate))).astype(BF16)

    for j in range(ncol):
        col_block(j)

    half = d_model // 2
    for c0 in (0, half):
        acc = x[:, c0:c0 + half]
        for j in range(ncol):
            acc = acc + jnp.dot(y_scr[j], wout_ref[j, :, c0:c0 + half],
                                preferred_element_type=F32)
        xout[slot, :, :, c0:c0 + half] = acc.reshape(tchunk, batch, half)
    for c in out_copies(step, slot):
        c.start()

    @pl.when(step == nsteps - 1)
    def _():
        for c in out_copies(step - 1, 1 - slot):
            c.wait()
        for c in out_copies(step, slot):
            c.wait()


def _rglru_layer(x, norm_g, w_in, conv_w, conv_b, w_rg, b_rg, w_ig, b_ig, lam, w_out):
    batch, seq, d_model = x.shape
    d_rnn = w_out.shape[0]
    assert d_rnn % RNN_COLS == 0 and w_rg.shape[1] == GATE_BW
    ncol = d_rnn // RNN_COLS
    rows = batch * TIME_CHUNK
    nsteps = seq // TIME_CHUNK
    assert seq % TIME_CHUNK == 0 and nsteps >= 2 and batch % 8 == 0
    halo = (CONV_WIDTH - 1) * batch

    def cols(w):
        lead = w.shape[:-1]
        w = w.reshape(lead + (ncol, RNN_COLS))
        return jnp.moveaxis(w, -2, 0)

    win = w_in.astype(BF16)
    convw = cols(conv_w)
    convb = cols(conv_b[None, :])
    wg = jnp.concatenate([w_rg, w_ig], axis=-1).astype(BF16)
    brg = cols(b_rg[None, :])
    big = cols(b_ig[None, :])
    lamc = cols(lam[None, :])
    wout = w_out.reshape(ncol, RNN_COLS, d_model).astype(BF16)

    full = lambda a: pl.BlockSpec(a.shape, lambda i: (0,) * a.ndim)
    kern = functools.partial(_rglru_kernel, batch=batch, tchunk=TIME_CHUNK, ncol=ncol,
                             nsteps=nsteps)
    return pl.pallas_call(
        kern,
        out_shape=jax.ShapeDtypeStruct((batch, seq, d_model), F32),
        grid=(nsteps,),
        in_specs=[
            pl.BlockSpec(memory_space=pl.ANY),
            full(norm_g), full(win), full(convw), full(convb), full(wg),
            full(brg), full(big), full(lamc), full(wout),
        ],
        out_specs=pl.BlockSpec(memory_space=pl.ANY),
        scratch_shapes=[
            pltpu.VMEM((2, TIME_CHUNK, batch, d_model), F32),
            pltpu.VMEM((2, TIME_CHUNK, batch, d_model), F32),
            pltpu.SemaphoreType.DMA((2,)),
            pltpu.SemaphoreType.DMA((2,)),
            pltpu.VMEM((rows, d_model), BF16),
            pltpu.VMEM((halo + rows, RNN_COLS), F32),
            pltpu.VMEM((ncol, rows, RNN_COLS), BF16),
            pltpu.VMEM((ncol, halo, RNN_COLS), F32),
            pltpu.VMEM((ncol, batch, RNN_COLS), F32),
        ],
        compiler_params=pltpu.CompilerParams(
            dimension_semantics=("arbitrary",), vmem_limit_bytes=VMEM_LIMIT_BYTES),
        name="rglru_layer",
    )(x, norm_g, win, convw, convb, wg, brg, big, lamc, wout)


_NT = (((1,), (1,)), ((), ()))


def _proj_kernel(x_ref, cs_ref, cst_ref, nkv_ref, nb_ref, wdkv_ref, wropet_ref, kvn_ref, wukt_ref,
                 wuv_ref, winq_ref, wing_ref, qn_ref, wuq_ref,
                 q_ref, kt_ref, v_ref, sg_ref, *, n_heads, kv_rank, nope, rope, v_dim, q_scale):
    x = x_ref[...]
    ms = jnp.mean(x * x, axis=-1, keepdims=True)
    xn = x * lax.rsqrt(ms + EPS)
    cs = cs_ref[...]

    def roped(tile):
        p = tile * cs
        return (p + pltpu.roll(p, rope, axis=1))[:, :rope]

    hk = (xn * nkv_ref[...]).astype(BF16)
    ckv = jnp.dot(hk, wdkv_ref[...], preferred_element_type=F32)
    c_kv = _rms(ckv, kvn_ref[...]).astype(BF16)
    pr = lax.dot_general(wropet_ref[...], hk, _NT, preferred_element_type=F32) * cst_ref[...]
    k_rope_t = (pr[:rope, :] + pr[rope:, :]).astype(BF16)
    k_nope_t = lax.dot_general(wukt_ref[...], c_kv, _NT, preferred_element_type=F32)
    v = jnp.dot(c_kv, wuv_ref[...], preferred_element_type=F32)
    for h in range(n_heads):
        kt_ref[h, 0:nope, :] = k_nope_t[h * nope:(h + 1) * nope, :].astype(BF16)
        kt_ref[h, nope:nope + rope, :] = k_rope_t
        v_ref[h] = v[:, h * v_dim:(h + 1) * v_dim].astype(BF16)

    hq = (xn * nb_ref[...]).astype(BF16)
    cq = jnp.dot(hq, winq_ref[...], preferred_element_type=F32)
    c_q = _rms(cq, qn_ref[...] * q_scale).astype(BF16)
    gate = jnp.dot(hq, wing_ref[...], preferred_element_type=F32)
    sg_ref[...] = (gate * _sigmoid(gate)).astype(BF16)
    qw = nope + 2 * rope
    for h in range(n_heads):
        qh = jnp.dot(c_q, wuq_ref[:, h * qw:(h + 1) * qw], preferred_element_type=F32)
        q_ref[h, :, 0:nope] = qh[:, :nope].astype(BF16)
        q_ref[h, :, nope:nope + rope] = roped(qh[:, nope:]).astype(BF16)


def _swap_halves(w):
    half = w.shape[-1] // 2
    return jnp.concatenate([w[..., half:], w[..., :half]], axis=-1)


def _mla_proj(x1, cs, norm_kv, w_dkv, kv_norm, w_uk, w_uv, norm_b, w_in, q_norm, w_uq,
              *, q_scale):
    bsz, seq, d_model = x1.shape
    kv_rank = kv_norm.shape[0]
    rope = w_dkv.shape[1] - kv_rank
    _, n_heads, nope = w_uk.shape
    v_dim = w_uv.shape[2]
    q_rank = q_norm.shape[0]
    assert 2 * rope == 128 and nope % 128 == 0 and seq % PROJ_ROWS == 0

    wr = w_dkv[:, kv_rank:]
    wdkv = w_dkv[:, :kv_rank].astype(BF16)
    wropet = jnp.concatenate([wr, _swap_halves(wr)], axis=1).T.astype(BF16)
    wukt = w_uk.reshape(kv_rank, n_heads * nope).T.astype(BF16)
    wuv = w_uv.reshape(kv_rank, n_heads * v_dim).astype(BF16)
    winq = w_in[:, :q_rank].astype(BF16)
    wing = w_in[:, q_rank:].astype(BF16)
    wqr = w_uq[:, :, nope:]
    wuq = jnp.concatenate([w_uq[:, :, :nope], wqr, _swap_halves(wqr)], axis=-1)
    wuq = wuq.reshape(q_rank, n_heads * (nope + 2 * rope)).astype(BF16)

    r2 = lambda a: a.reshape(1, -1)
    args = (x1, cs, cs.T, r2(norm_kv), r2(norm_b), wdkv, wropet, r2(kv_norm), wukt, wuv, winq,
            wing, r2(q_norm), wuq)
    full = lambda a: pl.BlockSpec(a.shape, lambda b, i: (0,) * a.ndim)
    qk = nope + rope
    kern = functools.partial(_proj_kernel, n_heads=n_heads, kv_rank=kv_rank, nope=nope,
                             rope=rope, v_dim=v_dim, q_scale=q_scale)
    head_out = lambda d: pl.BlockSpec((None, n_heads, PROJ_ROWS, d), lambda b, i: (b, 0, i, 0))
    return pl.pallas_call(
        kern,
        out_shape=(
            jax.ShapeDtypeStruct((bsz, n_heads, seq, qk), BF16),
            jax.ShapeDtypeStruct((bsz, n_heads, qk, seq), BF16),
            jax.ShapeDtypeStruct((bsz, n_heads, seq, v_dim), BF16),
            jax.ShapeDtypeStruct((bsz, seq, n_heads * v_dim), BF16),
        ),
        grid=(bsz, seq // PROJ_ROWS),
        in_specs=[
            pl.BlockSpec((None, PROJ_ROWS, d_model), lambda b, i: (b, i, 0)),
            pl.BlockSpec((PROJ_ROWS, 2 * rope), lambda b, i: (i, 0)),
            pl.BlockSpec((2 * rope, PROJ_ROWS), lambda b, i: (0, i)),
        ] + [full(a) for a in args[3:]],
        out_specs=(
            head_out(qk),
            pl.BlockSpec((None, n_heads, qk, PROJ_ROWS), lambda b, i: (b, 0, 0, i)),
            head_out(v_dim),
            pl.BlockSpec((None, PROJ_ROWS, n_heads * v_dim), lambda b, i: (b, i, 0)),
        ),
        compiler_params=pltpu.CompilerParams(
            dimension_semantics=("parallel", "parallel"), vmem_limit_bytes=VMEM_LIMIT_BYTES),
        name="mla_proj",
    )(*args)


def _attn_kernel(q_ref, kt_ref, v_ref, sg_ref, x_ref, wout_ref, fn_ref, o_ref,
                 m_scr, acc_scr, *, n_heads, tq):
    qi = pl.program_id(1)
    half = tq // 2

    def attend(r0, nrows, kstart, width, diag_off, first):
        if diag_off is not None:
            row = lax.broadcasted_iota(jnp.int32, (nrows, width), 0)
            col = lax.broadcasted_iota(jnp.int32, (nrows, width), 1)
            keep = col <= row + diag_off
        ones = jnp.ones((width, LANES), BF16)
        for h in range(n_heads):
            q = q_ref[h, r0:r0 + nrows, :]
            kt = kt_ref[h, :, pl.ds(kstart, width)]
            v = jnp.concatenate([v_ref[h, pl.ds(kstart, width), :], ones], axis=1)
            s = jnp.dot(q, kt, preferred_element_type=F32)
            if diag_off is not None:
                s = jnp.where(keep, s, NEG_BIG)
            m_cur = jnp.max(s, axis=-1, keepdims=True)
            if first:
                m_new = jnp.broadcast_to(m_cur, (nrows, LANES))
            else:
                m_prev = m_scr[h, r0:r0 + nrows, :]
                m_new = jnp.maximum(m_prev, m_cur)
            p = jnp.exp2(s - jnp.concatenate([m_new] * (width // LANES), axis=1))
            pv = jnp.dot(p.astype(BF16), v, preferred_element_type=F32)
            if not first:
                alpha = jnp.exp2(m_prev - m_new)
                pv = jnp.concatenate([alpha, alpha], axis=1) * acc_scr[h, r0:r0 + nrows, :] + pv
            acc_scr[h, r0:r0 + nrows, :] = pv
            m_scr[h, r0:r0 + nrows, :] = m_new

    diag = pl.multiple_of(qi * tq, tq)
    attend(0, tq, diag, half, 0, first=True)
    attend(half, half, pl.multiple_of(diag + half, half), half, 0, first=False)

    def full_pair(j, carry):
        attend(0, tq, pl.multiple_of(2 * j * tq, tq), tq, None, first=False)
        attend(0, tq, pl.multiple_of((2 * j + 1) * tq, tq), tq, None, first=False)
        return carry

    lax.fori_loop(0, qi // 2, full_pair, 0)

    @pl.when(qi % 2 == 1)
    def _():
        attend(0, tq, pl.multiple_of((qi - 1) * tq, tq), tq, None, first=False)

    o = jnp.concatenate([acc_scr[h, :, :LANES] / acc_scr[h, :, LANES:] for h in range(n_heads)],
                        axis=1)
    y = (o * sg_ref[...].astype(F32)).astype(BF16)
    out = x_ref[...] + jnp.dot(y, wout_ref[...], preferred_element_type=F32)
    o_ref[...] = _rms(out, fn_ref[...])


def _mla_attn(q, kt, v, sg, x1, w_out, final_norm):
    bsz, n_heads, seq, qk = q.shape
    v_dim = v.shape[-1]
    d_model = x1.shape[-1]
    tq = ATTN_TQ
    assert seq % tq == 0
    assert v_dim == LANES and tq % LANES == 0
    kern = functools.partial(_attn_kernel, n_heads=n_heads, tq=tq)
    wout = w_out.astype(BF16)
    fn = final_norm.reshape(1, -1)
    return pl.pallas_call(
        kern,
        out_shape=jax.ShapeDtypeStruct((bsz, seq, d_model), F32),
        grid=(bsz, seq // tq),
        in_specs=[
            pl.BlockSpec((None, n_heads, tq, qk), lambda b, i: (b, 0, i, 0)),
            pl.BlockSpec((None, n_heads, qk, seq), lambda b, i: (b, 0, 0, 0)),
            pl.BlockSpec((None, n_heads, seq, v_dim), lambda b, i: (b, 0, 0, 0)),
            pl.BlockSpec((None, tq, n_heads * v_dim), lambda b, i: (b, i, 0)),
            pl.BlockSpec((None, tq, d_model), lambda b, i: (b, i, 0)),
            pl.BlockSpec(wout.shape, lambda b, i: (0, 0)),
            pl.BlockSpec(fn.shape, lambda b, i: (0, 0)),
        ],
        out_specs=pl.BlockSpec((None, tq, d_model), lambda b, i: (b, i, 0)),
        scratch_shapes=[
            pltpu.VMEM((n_heads, tq, LANES), F32),
            pltpu.VMEM((n_heads, tq, v_dim + LANES), F32),
        ],
        compiler_params=pltpu.CompilerParams(
            dimension_semantics=("parallel", "arbitrary"), vmem_limit_bytes=VMEM_LIMIT_BYTES),
        name="mla_attn",
    )(q, kt, v, sg, x1, wout, fn)


def _rope_table(seq, rope):
    pos = jnp.arange(seq, dtype=F32)
    inv = ROPE_THETA ** (-jnp.arange(0, rope, 2, dtype=F32) / rope)
    ang = pos[:, None] * inv[None, :]
    cos, sin = jnp.cos(ang), jnp.sin(ang)
    return jnp.concatenate([cos, cos, -sin, sin], axis=1)


def kernel(x, norm_a, w_in_a, conv_w, conv_b, w_rg, b_rg, w_ig, b_ig, lru_lambda, w_out_a,
           norm_kv, w_dkv, kv_norm, w_uk, w_uv, norm_b, w_in_b, q_norm, w_uq, w_out_b,
           final_norm):
    bsz, seq, d_model = x.shape
    n_a = w_in_a.shape[0]
    n_b = w_in_b.shape[0]
    assert n_b == 1, "the final norm is fused into the (single) MLA layer"
    nope = w_uk.shape[-1]
    rope = w_dkv.shape[1] - kv_norm.shape[0]
    scale = float(nope + rope) ** -0.5

    x1 = x
    for layer in range(n_a):
        x1 = _rglru_layer(x1, norm_a[layer].reshape(1, -1), w_in_a[layer], conv_w[layer],
                          conv_b[layer], w_rg[layer], b_rg[layer], w_ig[layer], b_ig[layer],
                          lru_lambda[layer], w_out_a[layer])

    cs = _rope_table(seq, rope)
    q, kt, v, sg = _mla_proj(x1, cs, norm_kv, w_dkv, kv_norm, w_uk, w_uv, norm_b[0],
                            w_in_b[0], q_norm[0], w_uq[0], q_scale=scale * LOG2_E)
    return _mla_attn(q, kt, v, sg, x1, w_out_b[0], final_norm)
```

```python
import functools

import jax
import jax.numpy as jnp
from jax import lax
from jax.experimental import pallas as pl
from jax.experimental.pallas import tpu as pltpu

F32 = jnp.float32
BF16 = jnp.bfloat16

EPS = 1e-6
LRU_C = 8.0
ROPE_THETA = 10000.0
CONV_WIDTH = 4
NEG_BIG = -1e30
LOG2_E = 1.4426950408889634
LANES = 128

VMEM_LIMIT_BYTES = 56 * 1024 * 1024

RNN_COLS = 256
GATE_BW = 128
TIME_CHUNK = 32
PROJ_ROWS = 1024
ATTN_TQ = 512


def _rms(x, g):
    ms = jnp.mean(x * x, axis=-1, keepdims=True)
    return x * lax.rsqrt(ms + EPS) * g


def _sigmoid(x):
    return 1.0 / (1.0 + jnp.exp2(x * (-LOG2_E)))


def _sqrt_unit(t):
    return jnp.where(t > 0.0, t * lax.rsqrt(t), 0.0)


def _rglru_kernel(x_hbm, norm_ref, win_ref, convw_ref, convb_ref, wg_ref,
                  brg_ref, big_ref, lam_ref, wout_ref, o_hbm,
                  xin, xout, in_sem, out_sem, h_scr, xbuf, y_scr, hist, hstate,
                  *, batch, tchunk, ncol, nsteps):
    rows = batch * tchunk
    halo = (CONV_WIDTH - 1) * batch
    d_rnn = ncol * RNN_COLS
    step = pl.program_id(0)
    slot = lax.rem(step, 2)

    def in_copies(s, sl):
        return [pltpu.make_async_copy(x_hbm.at[b, pl.ds(s * tchunk, tchunk), :],
                                      xin.at[sl, :, b, :], in_sem.at[sl]) for b in range(batch)]

    def out_copies(s, sl):
        return [pltpu.make_async_copy(xout.at[sl, :, b, :],
                                      o_hbm.at[b, pl.ds(s * tchunk, tchunk), :], out_sem.at[sl])
                for b in range(batch)]

    @pl.when(step == 0)
    def _():
        hist[...] = jnp.zeros_like(hist)
        hstate[...] = jnp.zeros_like(hstate)
        for c in in_copies(0, 0):
            c.start()

    @pl.when(step + 1 < nsteps)
    def _():
        for c in in_copies(step + 1, 1 - slot):
            c.start()

    @pl.when(step >= 2)
    def _():
        for c in out_copies(step - 2, slot):
            c.wait()

    for c in in_copies(step, slot):
        c.wait()

    d_model = xin.shape[-1]
    x = xin[slot].reshape(rows, d_model)
    h_scr[...] = _rms(x, norm_ref[...]).astype(BF16)

    def col_block(j):
        hb = h_scr[...]
        c0 = j * RNN_COLS
        xb = jnp.dot(hb, win_ref[:, c0:c0 + RNN_COLS], preferred_element_type=F32)
        xbuf[0:halo, :] = hist[j]
        xbuf[halo:halo + rows, :] = xb
        hist[j] = xb[rows - halo:, :]
        cw = convw_ref[j]
        xc = convb_ref[j] + cw[3:4, :] * xb
        for k in range(CONV_WIDTH - 1):
            xc = xc + cw[k:k + 1, :] * xbuf[k * batch:k * batch + rows, :]
        xcb = xc.astype(BF16)
        g0 = jnp.dot(xcb[:, :GATE_BW], wg_ref[2 * j], preferred_element_type=F32)
        g1 = jnp.dot(xcb[:, GATE_BW:], wg_ref[2 * j + 1], preferred_element_type=F32)
        r_pre = jnp.concatenate([g0[:, :GATE_BW], g1[:, :GATE_BW]], axis=1) + brg_ref[j]
        i_pre = jnp.concatenate([g0[:, GATE_BW:], g1[:, GATE_BW:]], axis=1) + big_ref[j]
        r = _sigmoid(r_pre)
        ig = _sigmoid(i_pre)
        z = -lam_ref[j]
        softplus = jnp.maximum(z, 0.0) + jnp.log1p(jnp.exp(-jnp.abs(z)))
        log_a = r * (-LRU_C * softplus)
        a = jnp.exp(log_a)
        bt = _sqrt_unit(1.0 - a * a) * (ig * xc)
        hst = hstate[j]
        hs = []
        for t in range(tchunk):
            sl = slice(t * batch, (t + 1) * batch)
            hst = a[sl, :] * hst + bt[sl, :]
            hs.append(hst)
        hstate[j] = hst
        hs = jnp.concatenate(hs, axis=0)
        gate = jnp.dot(hb, win_ref[:, d_rnn + c0:d_rnn + c0 + RNN_COLS],
                       preferred_element_type=F32)
        y_scr[j] = (hs * (gate * _sigmoid(gate))).astype(BF16)

    for j in range(ncol):
        col_block(j)

    half = d_model // 2
    for c0 in (0, half):
        acc = x[:, c0:c0 + half]
        for j in range(ncol):
            acc = acc + jnp.dot(y_scr[j], wout_ref[j, :, c0:c0 + half],
                                preferred_element_type=F32)
        xout[slot, :, :, c0:c0 + half] = acc.reshape(tchunk, batch, half)
    for c in out_copies(step, slot):
        c.start()

    @pl.when(step == nsteps - 1)
    def _():
        for c in out_copies(step - 1, 1 - slot):
            c.wait()
        for c in out_copies(step, slot):
            c.wait()


def _rglru_layer(x, norm_g, w_in, conv_w, conv_b, w_rg, b_rg, w_ig, b_ig, lam, w_out):
    batch, seq, d_model = x.shape
    d_rnn = w_out.shape[0]
    assert d_rnn % RNN_COLS == 0 and w_rg.shape[1] == GATE_BW
    ncol = d_rnn // RNN_COLS
    rows = batch * TIME_CHUNK
    nsteps = seq // TIME_CHUNK
    assert seq % TIME_CHUNK == 0 and nsteps >= 2 and batch % 8 == 0
    halo = (CONV_WIDTH - 1) * batch

    def cols(w):
        lead = w.shape[:-1]
        w = w.reshape(lead + (ncol, RNN_COLS))
        return jnp.moveaxis(w, -2, 0)

    win = w_in.astype(BF16)
    convw = cols(conv_w)
    convb = cols(conv_b[None, :])
    wg = jnp.concatenate([w_rg, w_ig], axis=-1).astype(BF16)
    brg = cols(b_rg[None, :])
    big = cols(b_ig[None, :])
    lamc = cols(lam[None, :])
    wout = w_out.reshape(ncol, RNN_COLS, d_model).astype(BF16)

    full = lambda a: pl.BlockSpec(a.shape, lambda i: (0,) * a.ndim)
    kern = functools.partial(_rglru_kernel, batch=batch, tchunk=TIME_CHUNK, ncol=ncol,
                             nsteps=nsteps)
    return pl.pallas_call(
        kern,
        out_shape=jax.ShapeDtypeStruct((batch, seq, d_model), F32),
        grid=(nsteps,),
        in_specs=[
            pl.BlockSpec(memory_space=pl.ANY),
            full(norm_g), full(win), full(convw), full(convb), full(wg),
            full(brg), full(big), full(lamc), full(wout),
        ],
        out_specs=pl.BlockSpec(memory_space=pl.ANY),
        scratch_shapes=[
            pltpu.VMEM((2, TIME_CHUNK, batch, d_model), F32),
            pltpu.VMEM((2, TIME_CHUNK, batch, d_model), F32),
            pltpu.SemaphoreType.DMA((2,)),
            pltpu.SemaphoreType.DMA((2,)),
            pltpu.VMEM((rows, d_model), BF16),
            pltpu.VMEM((halo + rows, RNN_COLS), F32),
            pltpu.VMEM((ncol, rows, RNN_COLS), BF16),
            pltpu.VMEM((ncol, halo, RNN_COLS), F32),
            pltpu.VMEM((ncol, batch, RNN_COLS), F32),
        ],
        compiler_params=pltpu.CompilerParams(
            dimension_semantics=("arbitrary",), vmem_limit_bytes=VMEM_LIMIT_BYTES),
        name="rglru_layer",
    )(x, norm_g, win, convw, convb, wg, brg, big, lamc, wout)


_NT = (((1,), (1,)), ((), ()))


def _proj_kernel(x_ref, cs_ref, cst_ref, nkv_ref, nb_ref, wdkv_ref, wropet_ref, kvn_ref, wukt_ref,
                 wuv_ref, winq_ref, wing_ref, qn_ref, wuq_ref,
                 q_ref, kt_ref, v_ref, sg_ref, *, n_heads, kv_rank, nope, rope, v_dim, q_scale):
    x = x_ref[...]
    ms = jnp.mean(x * x, axis=-1, keepdims=True)
    xn = x * lax.rsqrt(ms + EPS)
    cs = cs_ref[...]

    def roped(tile):
        p = tile * cs
        return (p + pltpu.roll(p, rope, axis=1))[:, :rope]

    hk = (xn * nkv_ref[...]).astype(BF16)
    ckv = jnp.dot(hk, wdkv_ref[...], preferred_element_type=F32)
    c_kv = _rms(ckv, kvn_ref[...]).astype(BF16)
    pr = lax.dot_general(wropet_ref[...], hk, _NT, preferred_element_type=F32) * cst_ref[...]
    k_rope_t = (pr[:rope, :] + pr[rope:, :]).astype(BF16)
    k_nope_t = lax.dot_general(wukt_ref[...], c_kv, _NT, preferred_element_type=F32)
    v = jnp.dot(c_kv, wuv_ref[...], preferred_element_type=F32)
    for h in range(n_heads):
        kt_ref[h, 0:nope, :] = k_nope_t[h * nope:(h + 1) * nope, :].astype(BF16)
        kt_ref[h, nope:nope + rope, :] = k_rope_t
        v_ref[h] = v[:, h * v_dim:(h + 1) * v_dim].astype(BF16)

    hq = (xn * nb_ref[...]).astype(BF16)
    cq = jnp.dot(hq, winq_ref[...], preferred_element_type=F32)
    c_q = _rms(cq, qn_ref[...] * q_scale).astype(BF16)
    gate = jnp.dot(hq, wing_ref[...], preferred_element_type=F32)
    sg_ref[...] = (gate * _sigmoid(gate)).astype(BF16)
    qw = nope + 2 * rope
    for h in range(n_heads):
        qh = jnp.dot(c_q, wuq_ref[:, h * qw:(h + 1) * qw], preferred_element_type=F32)
        q_ref[h, :, 0:nope] = qh[:, :nope].astype(BF16)
        q_ref[h, :, nope:nope + rope] = roped(qh[:, nope:]).astype(BF16)


def _swap_halves(w):
    half = w.shape[-1] // 2
    return jnp.concatenate([w[..., half:], w[..., :half]], axis=-1)


def _mla_proj(x1, cs, norm_kv, w_dkv, kv_norm, w_uk, w_uv, norm_b, w_in, q_norm, w_uq,
              *, q_scale):
    bsz, seq, d_model = x1.shape
    kv_rank = kv_norm.shape[0]
    rope = w_dkv.shape[1] - kv_rank
    _, n_heads, nope = w_uk.shape
    v_dim = w_uv.shape[2]
    q_rank = q_norm.shape[0]
    assert 2 * rope == 128 and nope % 128 == 0 and seq % PROJ_ROWS == 0

    wr = w_dkv[:, kv_rank:]
    wdkv = w_dkv[:, :kv_rank].astype(BF16)
    wropet = jnp.concatenate([wr, _swap_halves(wr)], axis=1).T.astype(BF16)
    wukt = w_uk.reshape(kv_rank, n_heads * nope).T.astype(BF16)
    wuv = w_uv.reshape(kv_rank, n_heads * v_dim).astype(BF16)
    winq = w_in[:, :q_rank].astype(BF16)
    wing = w_in[:, q_rank:].astype(BF16)
    wqr = w_uq[:, :, nope:]
    wuq = jnp.concatenate([w_uq[:, :, :nope], wqr, _swap_halves(wqr)], axis=-1)
    wuq = wuq.reshape(q_rank, n_heads * (nope + 2 * rope)).astype(BF16)

    r2 = lambda a: a.reshape(1, -1)
    args = (x1, cs, cs.T, r2(norm_kv), r2(norm_b), wdkv, wropet, r2(kv_norm), wukt, wuv, winq,
            wing, r2(q_norm), wuq)
    full = lambda a: pl.BlockSpec(a.shape, lambda b, i: (0,) * a.ndim)
    qk = nope + rope
    kern = functools.partial(_proj_kernel, n_heads=n_heads, kv_rank=kv_rank, nope=nope,
                             rope=rope, v_dim=v_dim, q_scale=q_scale)
    head_out = lambda d: pl.BlockSpec((None, n_heads, PROJ_ROWS, d), lambda b, i: (b, 0, i, 0))
    return pl.pallas_call(
        kern,
        out_shape=(
            jax.ShapeDtypeStruct((bsz, n_heads, seq, qk), BF16),
            jax.ShapeDtypeStruct((bsz, n_heads, qk, seq), BF16),
            jax.ShapeDtypeStruct((bsz, n_heads, seq, v_dim), BF16),
            jax.ShapeDtypeStruct((bsz, seq, n_heads * v_dim), BF16),
        ),
        grid=(bsz, seq // PROJ_ROWS),
        in_specs=[
            pl.BlockSpec((None, PROJ_ROWS, d_model), lambda b, i: (b, i, 0)),
            pl.BlockSpec((PROJ_ROWS, 2 * rope), lambda b, i: (i, 0)),
            pl.BlockSpec((2 * rope, PROJ_ROWS), lambda b, i: (0, i)),
        ] + [full(a) for a in args[3:]],
        out_specs=(
            head_out(qk),
            pl.BlockSpec((None, n_heads, qk, PROJ_ROWS), lambda b, i: (b, 0, 0, i)),
            head_out(v_dim),
            pl.BlockSpec((None, PROJ_ROWS, n_heads * v_dim), lambda b, i: (b, i, 0)),
        ),
        compiler_params=pltpu.CompilerParams(
            dimension_semantics=("parallel", "parallel"), vmem_limit_bytes=VMEM_LIMIT_BYTES),
        name="mla_proj",
    )(*args)


def _attn_kernel(q_ref, kt_ref, v_ref, sg_ref, x_ref, wout_ref, fn_ref, o_ref,
                 m_scr, acc_scr, *, n_heads, tq, n_q_tiles):
    qi = pl.program_id(1)
    half = tq // 2

    def attend(r0, nrows, kstart, width, diag_off, first):
        if diag_off is not None:
            row = lax.broadcasted_iota(jnp.int32, (nrows, width), 0)
            col = lax.broadcasted_iota(jnp.int32, (nrows, width), 1)
            keep = col <= row + diag_off
        ones = jnp.ones((width, LANES), BF16)
        for h in range(n_heads):
            q = q_ref[h, r0:r0 + nrows, :]
            kt = kt_ref[h, :, pl.ds(kstart, width)]
            v = jnp.concatenate([v_ref[h, pl.ds(kstart, width), :], ones], axis=1)
            s = jnp.dot(q, kt, preferred_element_type=F32)
            if diag_off is not None:
                s = jnp.where(keep, s, NEG_BIG)
            m_cur = jnp.max(s, axis=-1, keepdims=True)
            if first:
                m_new = jnp.broadcast_to(m_cur, (nrows, LANES))
            else:
                m_prev = m_scr[h, r0:r0 + nrows, :]
                m_new = jnp.maximum(m_prev, m_cur)
            p = jnp.exp2(s - jnp.concatenate([m_new] * (width // LANES), axis=1))
            pv = jnp.dot(p.astype(BF16), v, preferred_element_type=F32)
            if not first:
                alpha = jnp.exp2(m_prev - m_new)
                pv = jnp.concatenate([alpha, alpha], axis=1) * acc_scr[h, r0:r0 + nrows, :] + pv
            acc_scr[h, r0:r0 + nrows, :] = pv
            m_scr[h, r0:r0 + nrows, :] = m_new

    for n_full in range(n_q_tiles):
        @pl.when(qi == n_full)
        def _(n_full=n_full):
            diag = n_full * tq
            attend(0, tq, diag, half, 0, first=True)
            attend(half, half, diag + half, half, 0, first=False)
            for j in range(n_full):
                attend(0, tq, j * tq, tq, None, first=False)

    o = jnp.concatenate([acc_scr[h, :, :LANES] / acc_scr[h, :, LANES:] for h in range(n_heads)],
                        axis=1)
    y = (o * sg_ref[...].astype(F32)).astype(BF16)
    out = x_ref[...] + jnp.dot(y, wout_ref[...], preferred_element_type=F32)
    o_ref[...] = _rms(out, fn_ref[...])


def _mla_attn(q, kt, v, sg, x1, w_out, final_norm):
    bsz, n_heads, seq, qk = q.shape
    v_dim = v.shape[-1]
    d_model = x1.shape[-1]
    tq = ATTN_TQ
    assert seq % tq == 0
    assert v_dim == LANES and tq % LANES == 0
    kern = functools.partial(_attn_kernel, n_heads=n_heads, tq=tq, n_q_tiles=seq // tq)
    wout = w_out.astype(BF16)
    fn = final_norm.reshape(1, -1)
    return pl.pallas_call(
        kern,
        out_shape=jax.ShapeDtypeStruct((bsz, seq, d_model), F32),
        grid=(bsz, seq // tq),
        in_specs=[
            pl.BlockSpec((None, n_heads, tq, qk), lambda b, i: (b, 0, i, 0)),
            pl.BlockSpec((None, n_heads, qk, seq), lambda b, i: (b, 0, 0, 0)),
            pl.BlockSpec((None, n_heads, seq, v_dim), lambda b, i: (b, 0, 0, 0)),
            pl.BlockSpec((None, tq, n_heads * v_dim), lambda b, i: (b, i, 0)),
            pl.BlockSpec((None, tq, d_model), lambda b, i: (b, i, 0)),
            pl.BlockSpec(wout.shape, lambda b, i: (0, 0)),
            pl.BlockSpec(fn.shape, lambda b, i: (0, 0)),
        ],
        out_specs=pl.BlockSpec((None, tq, d_model), lambda b, i: (b, i, 0)),
        scratch_shapes=[
            pltpu.VMEM((n_heads, tq, LANES), F32),
            pltpu.VMEM((n_heads, tq, v_dim + LANES), F32),
        ],
        compiler_params=pltpu.CompilerParams(
            dimension_semantics=("parallel", "arbitrary"), vmem_limit_bytes=VMEM_LIMIT_BYTES),
        name="mla_attn",
    )(q, kt, v, sg, x1, wout, fn)


def _rope_table(seq, rope):
    pos = jnp.arange(seq, dtype=F32)
    inv = ROPE_THETA ** (-jnp.arange(0, rope, 2, dtype=F32) / rope)
    ang = pos[:, None] * inv[None, :]
    cos, sin = jnp.cos(ang), jnp.sin(ang)
    return jnp.concatenate([cos, cos, -sin, sin], axis=1)


def kernel(x, norm_a, w_in_a, conv_w, conv_b, w_rg, b_rg, w_ig, b_ig, lru_lambda, w_out_a,
           norm_kv, w_dkv, kv_norm, w_uk, w_uv, norm_b, w_in_b, q_norm, w_uq, w_out_b,
           final_norm):
    bsz, seq, d_model = x.shape
    n_a = w_in_a.shape[0]
    n_b = w_in_b.shape[0]
    assert n_b == 1, "the final norm is fused into the (single) MLA layer"
    nope = w_uk.shape[-1]
    rope = w_dkv.shape[1] - kv_norm.shape[0]
    scale = float(nope + rope) ** -0.5

    x1 = x
    for layer in range(n_a):
        x1 = _rglru_layer(x1, norm_a[layer].reshape(1, -1), w_in_a[layer], conv_w[layer],
                          conv_b[layer], w_rg[layer], b_rg[layer], w_ig[layer], b_ig[layer],
                          lru_lambda[layer], w_out_a[layer])

    cs = _rope_table(seq, rope)
    q, kt, v, sg = _mla_proj(x1, cs, norm_kv, w_dkv, kv_norm, w_uk, w_uv, norm_b[0],
                            w_in_b[0], q_norm[0], w_uq[0], q_scale=scale * LOG2_E)
    return _mla_attn(q, kt, v, sg, x1, w_out_b[0], final_norm)
```

```python
import functools

import jax
import jax.numpy as jnp
from jax import lax
from jax.experimental import pallas as pl
from jax.experimental.pallas import tpu as pltpu

F32 = jnp.float32
BF16 = jnp.bfloat16

EPS = 1e-6
LRU_C = 8.0
ROPE_THETA = 10000.0
CONV_WIDTH = 4
NEG_BIG = -1e30
LOG2_E = 1.4426950408889634
LANES = 128

VMEM_LIMIT_BYTES = 56 * 1024 * 1024

RNN_COLS = 256
GATE_BW = 128
TIME_CHUNK = 32
PROJ_ROWS = 1024
ATTN_TQ = 512


def _rms(x, g):
    ms = jnp.mean(x * x, axis=-1, keepdims=True)
    return x * lax.rsqrt(ms + EPS) * g


def _sigmoid(x):
    return 1.0 / (1.0 + jnp.exp2(x * (-LOG2_E)))


def _sqrt_unit(t):
    return jnp.where(t > 0.0, t * lax.rsqrt(t), 0.0)


def _rglru_kernel(x_hbm, norm_ref, win_ref, convw_ref, convb_ref, wg_ref,
                  brg_ref, big_ref, lam_ref, wout_ref, o_hbm,
                  xin, xout, in_sem, out_sem, h_scr, xbuf, y_scr, hist, hstate,
                  *, batch, tchunk, ncol, nsteps):
    rows = batch * tchunk
    halo = (CONV_WIDTH - 1) * batch
    d_rnn = ncol * RNN_COLS
    step = pl.program_id(0)
    slot = lax.rem(step, 2)

    def in_copies(s, sl):
        return [pltpu.make_async_copy(x_hbm.at[b, pl.ds(s * tchunk, tchunk), :],
                                      xin.at[sl, :, b, :], in_sem.at[sl]) for b in range(batch)]

    def out_copies(s, sl):
        return [pltpu.make_async_copy(xout.at[sl, :, b, :],
                                      o_hbm.at[b, pl.ds(s * tchunk, tchunk), :], out_sem.at[sl])
                for b in range(batch)]

    @pl.when(step == 0)
    def _():
        hist[...] = jnp.zeros_like(hist)
        hstate[...] = jnp.zeros_like(hstate)
        for c in in_copies(0, 0):
            c.start()

    @pl.when(step + 1 < nsteps)
    def _():
        for c in in_copies(step + 1, 1 - slot):
            c.start()

    @pl.when(step >= 2)
    def _():
        for c in out_copies(step - 2, slot):
            c.wait()

    for c in in_copies(step, slot):
        c.wait()

    d_model = xin.shape[-1]
    x = xin[slot].reshape(rows, d_model)
    h_scr[...] = _rms(x, norm_ref[...]).astype(BF16)

    def col_block(j):
        hb = h_scr[...]
        c0 = j * RNN_COLS
        xb = jnp.dot(hb, win_ref[:, c0:c0 + RNN_COLS], preferred_element_type=F32)
        xbuf[0:halo, :] = hist[j]
        xbuf[halo:halo + rows, :] = xb
        hist[j] = xb[rows - halo:, :]
        cw = convw_ref[j]
        xc = convb_ref[j] + cw[3:4, :] * xb
        for k in range(CONV_WIDTH - 1):
            xc = xc + cw[k:k + 1, :] * xbuf[k * batch:k * batch + rows, :]
        xcb = xc.astype(BF16)
        g0 = jnp.dot(xcb[:, :GATE_BW], wg_ref[2 * j], preferred_element_type=F32)
        g1 = jnp.dot(xcb[:, GATE_BW:], wg_ref[2 * j + 1], preferred_element_type=F32)
        r_pre = jnp.concatenate([g0[:, :GATE_BW], g1[:, :GATE_BW]], axis=1) + brg_ref[j]
        i_pre = jnp.concatenate([g0[:, GATE_BW:], g1[:, GATE_BW:]], axis=1) + big_ref[j]
        r = _sigmoid(r_pre)
        ig = _sigmoid(i_pre)
        z = -lam_ref[j]
        softplus = jnp.maximum(z, 0.0) + jnp.log1p(jnp.exp(-jnp.abs(z)))
        log_a = r * (-LRU_C * softplus)
        a = jnp.exp(log_a)
        bt = _sqrt_unit(1.0 - a * a) * (ig * xc)
        hst = hstate[j]
        hs = []
        for t in range(tchunk):
            sl = slice(t * batch, (t + 1) * batch)
            hst = a[sl, :] * hst + bt[sl, :]
            hs.append(hst)
        hstate[j] = hst
        hs = jnp.concatenate(hs, axis=0)
        gate = jnp.dot(hb, win_ref[:, d_rnn + c0:d_rnn + c0 + RNN_COLS],
                       preferred_element_type=F32)
        y_scr[j] = (hs * (gate * _sigmoid(gate))).astype(BF16)

    for j in range(ncol):
        col_block(j)

    half = d_model // 2
    for c0 in (0, half):
        acc = x[:, c0:c0 + half]
        for j in range(ncol):
            acc = acc + jnp.dot(y_scr[j], wout_ref[j, :, c0:c0 + half],
                                preferred_element_type=F32)
        xout[slot, :, :, c0:c0 + half] = acc.reshape(tchunk, batch, half)
    for c in out_copies(step, slot):
        c.start()

    @pl.when(step == nsteps - 1)
    def _():
        for c in out_copies(step - 1, 1 - slot):
            c.wait()
        for c in out_copies(step, slot):
            c.wait()


def _rglru_layer(x, norm_g, w_in, conv_w, conv_b, w_rg, b_rg, w_ig, b_ig, lam, w_out):
    batch, seq, d_model = x.shape
    d_rnn = w_out.shape[0]
    assert d_rnn % RNN_COLS == 0 and w_rg.shape[1] == GATE_BW
    ncol = d_rnn // RNN_COLS
    rows = batch * TIME_CHUNK
    nsteps = seq // TIME_CHUNK
    assert seq % TIME_CHUNK == 0 and nsteps >= 2 and batch % 8 == 0
    halo = (CONV_WIDTH - 1) * batch

    def cols(w):
        lead = w.shape[:-1]
        w = w.reshape(lead + (ncol, RNN_COLS))
        return jnp.moveaxis(w, -2, 0)

    win = w_in.astype(BF16)
    convw = cols(conv_w)
    convb = cols(conv_b[None, :])
    wg = jnp.concatenate([w_rg, w_ig], axis=-1).astype(BF16)
    brg = cols(b_rg[None, :])
    big = cols(b_ig[None, :])
    lamc = cols(lam[None, :])
    wout = w_out.reshape(ncol, RNN_COLS, d_model).astype(BF16)

    full = lambda a: pl.BlockSpec(a.shape, lambda i: (0,) * a.ndim)
    kern = functools.partial(_rglru_kernel, batch=batch, tchunk=TIME_CHUNK, ncol=ncol,
                             nsteps=nsteps)
    return pl.pallas_call(
        kern,
        out_shape=jax.ShapeDtypeStruct((batch, seq, d_model), F32),
        grid=(nsteps,),
        in_specs=[
            pl.BlockSpec(memory_space=pl.ANY),
            full(norm_g), full(win), full(convw), full(convb), full(wg),
            full(brg), full(big), full(lamc), full(wout),
        ],
        out_specs=pl.BlockSpec(memory_space=pl.ANY),
        scratch_shapes=[
            pltpu.VMEM((2, TIME_CHUNK, batch, d_model), F32),
            pltpu.VMEM((2, TIME_CHUNK, batch, d_model), F32),
            pltpu.SemaphoreType.DMA((2,)),
            pltpu.SemaphoreType.DMA((2,)),
            pltpu.VMEM((rows, d_model), BF16),
            pltpu.VMEM((halo + rows, RNN_COLS), F32),
            pltpu.VMEM((ncol, rows, RNN_COLS), BF16),
            pltpu.VMEM((ncol, halo, RNN_COLS), F32),
            pltpu.VMEM((ncol, batch, RNN_COLS), F32),
        ],
        compiler_params=pltpu.CompilerParams(
            dimension_semantics=("arbitrary",), vmem_limit_bytes=VMEM_LIMIT_BYTES),
        name="rglru_layer",
    )(x, norm_g, win, convw, convb, wg, brg, big, lamc, wout)


_NT = (((1,), (1,)), ((), ()))


def _proj_kernel(x_ref, cs_ref, cst_ref, nkv_ref, nb_ref, wdkv_ref, wropet_ref, kvn_ref, wukt_ref,
                 wuv_ref, winq_ref, wing_ref, qn_ref, wuq_ref,
                 q_ref, kt_ref, v_ref, sg_ref, *, n_heads, nope, rope, v_dim, q_scale):
    x = x_ref[...]
    ms = jnp.mean(x * x, axis=-1, keepdims=True)
    xn = x * lax.rsqrt(ms + EPS)
    cs = cs_ref[...]

    def roped(tile):
        p = tile * cs
        return (p + pltpu.roll(p, rope, axis=1))[:, :rope]

    hk = (xn * nkv_ref[...]).astype(BF16)
    ckv = jnp.dot(hk, wdkv_ref[...], preferred_element_type=F32)
    c_kv = _rms(ckv, kvn_ref[...]).astype(BF16)
    pr = lax.dot_general(wropet_ref[...], hk, _NT, preferred_element_type=F32) * cst_ref[...]
    k_rope_t = (pr[:rope, :] + pr[rope:, :]).astype(BF16)
    k_nope_t = lax.dot_general(wukt_ref[...], c_kv, _NT, preferred_element_type=F32)
    v = jnp.dot(c_kv, wuv_ref[...], preferred_element_type=F32)
    for h in range(n_heads):
        kt_ref[h, 0:nope, :] = k_nope_t[h * nope:(h + 1) * nope, :].astype(BF16)
        kt_ref[h, nope:nope + rope, :] = k_rope_t
        v_ref[h] = v[:, h * v_dim:(h + 1) * v_dim].astype(BF16)

    hq = (xn * nb_ref[...]).astype(BF16)
    cq = jnp.dot(hq, winq_ref[...], preferred_element_type=F32)
    c_q = _rms(cq, qn_ref[...] * q_scale).astype(BF16)
    gate = jnp.dot(hq, wing_ref[...], preferred_element_type=F32)
    sg_ref[...] = (gate * _sigmoid(gate)).astype(BF16)
    qw = nope + 2 * rope
    for h in range(n_heads):
        qh = jnp.dot(c_q, wuq_ref[:, h * qw:(h + 1) * qw], preferred_element_type=F32)
        q_ref[h, :, 0:nope] = qh[:, :nope].astype(BF16)
        q_ref[h, :, nope:nope + rope] = roped(qh[:, nope:]).astype(BF16)


def _swap_halves(w):
    half = w.shape[-1] // 2
    return jnp.concatenate([w[..., half:], w[..., :half]], axis=-1)


def _mla_proj(x1, cs, norm_kv, w_dkv, kv_norm, w_uk, w_uv, norm_b, w_in, q_norm, w_uq,
              *, q_scale):
    bsz, seq, d_model = x1.shape
    kv_rank = kv_norm.shape[0]
    rope = w_dkv.shape[1] - kv_rank
    _, n_heads, nope = w_uk.shape
    v_dim = w_uv.shape[2]
    q_rank = q_norm.shape[0]
    assert 2 * rope == LANES and nope % LANES == 0 and seq % PROJ_ROWS == 0

    wr = w_dkv[:, kv_rank:]
    wdkv = w_dkv[:, :kv_rank].astype(BF16)
    wropet = jnp.concatenate([wr, _swap_halves(wr)], axis=1).T.astype(BF16)
    wukt = w_uk.reshape(kv_rank, n_heads * nope).T.astype(BF16)
    wuv = w_uv.reshape(kv_rank, n_heads * v_dim).astype(BF16)
    winq = w_in[:, :q_rank].astype(BF16)
    wing = w_in[:, q_rank:].astype(BF16)
    wqr = w_uq[:, :, nope:]
    wuq = jnp.concatenate([w_uq[:, :, :nope], wqr, _swap_halves(wqr)], axis=-1)
    wuq = wuq.reshape(q_rank, n_heads * (nope + 2 * rope)).astype(BF16)

    r2 = lambda a: a.reshape(1, -1)
    args = (x1, cs, cs.T, r2(norm_kv), r2(norm_b), wdkv, wropet, r2(kv_norm), wukt, wuv, winq,
            wing, r2(q_norm), wuq)
    full = lambda a: pl.BlockSpec(a.shape, lambda b, i: (0,) * a.ndim)
    qk = nope + rope
    kern = functools.partial(_proj_kernel, n_heads=n_heads, nope=nope, rope=rope, v_dim=v_dim,
                             q_scale=q_scale)
    head_out = lambda d: pl.BlockSpec((None, n_heads, PROJ_ROWS, d), lambda b, i: (b, 0, i, 0))
    return pl.pallas_call(
        kern,
        out_shape=(
            jax.ShapeDtypeStruct((bsz, n_heads, seq, qk), BF16),
            jax.ShapeDtypeStruct((bsz, n_heads, qk, seq), BF16),
            jax.ShapeDtypeStruct((bsz, n_heads, seq, v_dim), BF16),
            jax.ShapeDtypeStruct((bsz, seq, n_heads * v_dim), BF16),
        ),
        grid=(bsz, seq // PROJ_ROWS),
        in_specs=[
            pl.BlockSpec((None, PROJ_ROWS, d_model), lambda b, i: (b, i, 0)),
            pl.BlockSpec((PROJ_ROWS, 2 * rope), lambda b, i: (i, 0)),
            pl.BlockSpec((2 * rope, PROJ_ROWS), lambda b, i: (0, i)),
        ] + [full(a) for a in args[3:]],
        out_specs=(
            head_out(qk),
            pl.BlockSpec((None, n_heads, qk, PROJ_ROWS), lambda b, i: (b, 0, 0, i)),
            head_out(v_dim),
            pl.BlockSpec((None, PROJ_ROWS, n_heads * v_dim), lambda b, i: (b, i, 0)),
        ),
        compiler_params=pltpu.CompilerParams(
            dimension_semantics=("parallel", "parallel"), vmem_limit_bytes=VMEM_LIMIT_BYTES),
        name="mla_proj",
    )(*args)


def _attn_kernel(q_ref, kt_ref, v_ref, sg_ref, x_ref, wout_ref, fn_ref, o_ref,
                 m_scr, acc_scr, *, n_heads, tq, n_q_tiles):
    qi = pl.program_id(1)
    half = tq // 2

    def attend(r0, nrows, kstart, width, diag_off, first):
        if diag_off is not None:
            row = lax.broadcasted_iota(jnp.int32, (nrows, width), 0)
            col = lax.broadcasted_iota(jnp.int32, (nrows, width), 1)
            keep = col <= row + diag_off
        ones = jnp.ones((width, LANES), BF16)
        for h in range(n_heads):
            q = q_ref[h, r0:r0 + nrows, :]
            kt = kt_ref[h, :, pl.ds(kstart, width)]
            v = jnp.concatenate([v_ref[h, pl.ds(kstart, width), :], ones], axis=1)
            s = jnp.dot(q, kt, preferred_element_type=F32)
            if diag_off is not None:
                s = jnp.where(keep, s, NEG_BIG)
            m_cur = jnp.max(s, axis=-1, keepdims=True)
            if first:
                m_new = jnp.broadcast_to(m_cur, (nrows, LANES))
            else:
                m_prev = m_scr[h, r0:r0 + nrows, :]
                m_new = jnp.maximum(m_prev, m_cur)
            p = jnp.exp2(s - jnp.concatenate([m_new] * (width // LANES), axis=1))
            pv = jnp.dot(p.astype(BF16), v, preferred_element_type=F32)
            if not first:
                alpha = jnp.exp2(m_prev - m_new)
                pv = jnp.concatenate([alpha, alpha], axis=1) * acc_scr[h, r0:r0 + nrows, :] + pv
            acc_scr[h, r0:r0 + nrows, :] = pv
            m_scr[h, r0:r0 + nrows, :] = m_new

    for n_full in range(n_q_tiles):
        @pl.when(qi == n_full)
        def _(n_full=n_full):
            diag = n_full * tq
            attend(0, tq, diag, half, 0, first=True)
            attend(half, half, diag + half, half, 0, first=False)
            for j in range(n_full):
                attend(0, tq, j * tq, tq, None, first=False)

    o = jnp.concatenate([acc_scr[h, :, :LANES] / acc_scr[h, :, LANES:] for h in range(n_heads)],
                        axis=1)
    y = (o * sg_ref[...].astype(F32)).astype(BF16)
    out = x_ref[...] + jnp.dot(y, wout_ref[...], preferred_element_type=F32)
    o_ref[...] = _rms(out, fn_ref[...])


def _mla_attn(q, kt, v, sg, x1, w_out, final_norm):
    bsz, n_heads, seq, qk = q.shape
    v_dim = v.shape[-1]
    d_model = x1.shape[-1]
    tq = ATTN_TQ
    assert seq % tq == 0
    assert v_dim == LANES and tq % LANES == 0
    kern = functools.partial(_attn_kernel, n_heads=n_heads, tq=tq, n_q_tiles=seq // tq)
    wout = w_out.astype(BF16)
    fn = final_norm.reshape(1, -1)
    return pl.pallas_call(
        kern,
        out_shape=jax.ShapeDtypeStruct((bsz, seq, d_model), F32),
        grid=(bsz, seq // tq),
        in_specs=[
            pl.BlockSpec((None, n_heads, tq, qk), lambda b, i: (b, 0, i, 0)),
            pl.BlockSpec((None, n_heads, qk, seq), lambda b, i: (b, 0, 0, 0)),
            pl.BlockSpec((None, n_heads, seq, v_dim), lambda b, i: (b, 0, 0, 0)),
            pl.BlockSpec((None, tq, n_heads * v_dim), lambda b, i: (b, i, 0)),
            pl.BlockSpec((None, tq, d_model), lambda b, i: (b, i, 0)),
            pl.BlockSpec(wout.shape, lambda b, i: (0, 0)),
            pl.BlockSpec(fn.shape, lambda b, i: (0, 0)),
        ],
        out_specs=pl.BlockSpec((None, tq, d_model), lambda b, i: (b, i, 0)),
        scratch_shapes=[
            pltpu.VMEM((n_heads, tq, LANES), F32),
            pltpu.VMEM((n_heads, tq, v_dim + LANES), F32),
        ],
        compiler_params=pltpu.CompilerParams(
            dimension_semantics=("parallel", "arbitrary"), vmem_limit_bytes=VMEM_LIMIT_BYTES),
        name="mla_attn",
    )(q, kt, v, sg, x1, wout, fn)


def _rope_table(seq, rope):
    pos = jnp.arange(seq, dtype=F32)
    inv = ROPE_THETA ** (-jnp.arange(0, rope, 2, dtype=F32) / rope)
    ang = pos[:, None] * inv[None, :]
    cos, sin = jnp.cos(ang), jnp.sin(ang)
    return jnp.concatenate([cos, cos, -sin, sin], axis=1)


def kernel(x, norm_a, w_in_a, conv_w, conv_b, w_rg, b_rg, w_ig, b_ig, lru_lambda, w_out_a,
           norm_kv, w_dkv, kv_norm, w_uk, w_uv, norm_b, w_in_b, q_norm, w_uq, w_out_b,
           final_norm):
    bsz, seq, d_model = x.shape
    n_a = w_in_a.shape[0]
    n_b = w_in_b.shape[0]
    assert n_b == 1, "the final norm is fused into the (single) MLA layer"
    nope = w_uk.shape[-1]
    rope = w_dkv.shape[1] - kv_norm.shape[0]
    scale = float(nope + rope) ** -0.5

    x1 = x
    for layer in range(n_a):
        x1 = _rglru_layer(x1, norm_a[layer].reshape(1, -1), w_in_a[layer], conv_w[layer],
                          conv_b[layer], w_rg[layer], b_rg[layer], w_ig[layer], b_ig[layer],
                          lru_lambda[layer], w_out_a[layer])

    cs = _rope_table(seq, rope)
    q, kt, v, sg = _mla_proj(x1, cs, norm_kv, w_dkv, kv_norm, w_uk, w_uv, norm_b[0],
                            w_in_b[0], q_norm[0], w_uq[0], q_scale=scale * LOG2_E)
    return _mla_attn(q, kt, v, sg, x1, w_out_b[0], final_norm)
```
